```python
import jax, jax.numpy as jnp
from jax import lax
import numpy as np

D_MODEL = 1024
BATCH = 4
SEQ = 8192
DEPTH = 2

LRU_WIDTH = D_MODEL
LRU_BLOCKS = 8
LRU_BLOCK_W = LRU_WIDTH // LRU_BLOCKS
CONV_W = 4
LRU_C = 8.0
MLA_HEADS = 8
QK_NOPE = 128
QK_ROPE = 64
QK_HEAD = QK_NOPE + QK_ROPE
V_HEAD = D_MODEL // MLA_HEADS
Q_RANK = 256
KV_RANK = 128
ROPE_THETA = 10000.0
Q_BLOCK = 128
D_FF = -(-8 * D_MODEL // (3 * 256)) * 256
EPS = 1e-6
D_IN = LRU_WIDTH + Q_RANK + KV_RANK + QK_ROPE + 2 * D_MODEL
IN_SPLIT_POINTS = (LRU_WIDTH,
                   LRU_WIDTH + Q_RANK,
                   LRU_WIDTH + Q_RANK + KV_RANK,
                   LRU_WIDTH + Q_RANK + KV_RANK + QK_ROPE,
                   LRU_WIDTH + Q_RANK + KV_RANK + QK_ROPE + D_MODEL)

kernel_name = 'hybrid_rglru_mla_adaln_block'


def rms_norm(x, gain=None):
    xf = x.astype(jnp.float32)
    y = xf * lax.rsqrt(jnp.mean(xf * xf, axis=-1, keepdims=True) + EPS)
    if gain is not None:
        y = y * gain.astype(jnp.float32)
    return y.astype(x.dtype)


def rope_tables(positions):
    inv_freq = ROPE_THETA ** (-jnp.arange(0, QK_ROPE, 2, dtype=jnp.float32) / QK_ROPE)
    ang = positions.astype(jnp.float32)[..., None] * inv_freq
    return jnp.cos(ang), jnp.sin(ang)


def apply_rope(x, cos, sin):
    half = QK_ROPE // 2
    xf = x.astype(jnp.float32)
    x1, x2 = xf[..., :half], xf[..., half:]
    return jnp.concatenate([x1 * cos - x2 * sin, x2 * cos + x1 * sin], axis=-1).astype(x.dtype)


def causal_depthwise_conv(x, w, b):
    y = lax.conv_general_dilated(x, w[:, None, :].astype(x.dtype), window_strides=(1,),
                                 padding=((CONV_W - 1, 0),),
                                 dimension_numbers=('NWC', 'WIO', 'NWC'),
                                 feature_group_count=x.shape[-1])
    return y + b


def rg_lru(x, positions, w_a, b_a, w_x, b_x, a_param):
    B, S, _ = x.shape
    xf = x.astype(jnp.float32)
    xb = xf.reshape(B, S, LRU_BLOCKS, LRU_BLOCK_W)
    r = jax.nn.sigmoid(jnp.einsum('bsni,nij->bsnj', xb, w_a.astype(jnp.float32))
                       + b_a.astype(jnp.float32)).reshape(B, S, LRU_WIDTH)
    i = jax.nn.sigmoid(jnp.einsum('bsni,nij->bsnj', xb, w_x.astype(jnp.float32))
                       + b_x.astype(jnp.float32)).reshape(B, S, LRU_WIDTH)
    log_a = -LRU_C * r * jax.nn.softplus(-a_param.astype(jnp.float32))
    reset = (positions == 0)[..., None]
    a = jnp.where(reset, 0.0, jnp.exp(log_a))
    mult = jnp.where(reset, 1.0, jnp.sqrt(-jnp.expm1(2.0 * log_a)))
    b_in = xf * i * mult

    def combine(left, right):
        return (left[0] * right[0], right[0] * left[1] + right[1])

    _, h = lax.associative_scan(combine, (a, b_in), axis=1)
    return h.astype(x.dtype)


def causal_block_attention(q, k, v):
    B, S, H, Dk = q.shape
    nb = S // Q_BLOCK
    scale = QK_HEAD ** -0.5
    qb = q.reshape(B, nb, Q_BLOCK, H, Dk).transpose(1, 0, 2, 3, 4)
    k_idx = jnp.arange(S)

    def one_block(args):
        q_blk, blk = args
        s = jnp.einsum('bqhd,bkhd->bhqk', q_blk, k, preferred_element_type=jnp.float32) * scale
        q_idx = blk * Q_BLOCK + jnp.arange(Q_BLOCK)
        s = jnp.where(k_idx[None, :] <= q_idx[:, None], s, -jnp.inf)
        p = jax.nn.softmax(s, axis=-1)
        o = jnp.einsum('bhqk,bkhd->bqhd', p.astype(v.dtype), v, preferred_element_type=jnp.float32)
        return o.astype(v.dtype)

    o = lax.map(one_block, (qb, jnp.arange(nb)))
    return o.transpose(1, 0, 2, 3, 4).reshape(B, S, H, v.shape[-1])


def mla(q_down, kv_down, k_rope, cos, sin, q_norm_g, kv_norm_g, w_uq, w_ukv):
    B, S, _ = q_down.shape
    c_q = rms_norm(q_down, q_norm_g)
    q = jnp.einsum('bsr,rhd->bshd', c_q, w_uq)
    q = jnp.concatenate([q[..., :QK_NOPE],
                         apply_rope(q[..., QK_NOPE:], cos[:, :, None, :], sin[:, :, None, :])], axis=-1)
    c_kv = rms_norm(kv_down, kv_norm_g)
    kv = jnp.einsum('bsr,rhd->bshd', c_kv, w_ukv)
    k_pe = apply_rope(k_rope, cos, sin)
    k = jnp.concatenate([kv[..., :QK_NOPE],
                         jnp.broadcast_to(k_pe[:, :, None, :], (B, S, MLA_HEADS, QK_ROPE))], axis=-1)
    v = kv[..., QK_NOPE:]
    o = causal_block_attention(q, k, v)
    return o.reshape(B, S, MLA_HEADS * V_HEAD)


def hybrid_layer(x, c, positions, cos, sin, w_ada, b_ada, w_in, conv_w, conv_b,
                 lru_wa, lru_ba, lru_wx, lru_bx, lru_a_param, q_norm_g, kv_norm_g,
                 w_uq, w_ukv, w_out, w_ffn_in, w_ffn_out):
    mod = (c @ w_ada + b_ada)[:, None, :]
    sh_m, sc_m, g_m, sh_f, sc_f, g_f = jnp.split(mod, 6, axis=-1)
    h = rms_norm(x) * (1 + sc_m) + sh_m
    proj = h @ w_in
    x_lru, q_down, kv_down, k_rope, gate_a, gate_b = jnp.split(proj, IN_SPLIT_POINTS, axis=-1)
    y_a = rg_lru(causal_depthwise_conv(x_lru, conv_w, conv_b), positions,
                 lru_wa, lru_ba, lru_wx, lru_bx, lru_a_param)
    y_b = mla(q_down, kv_down, k_rope, cos, sin, q_norm_g, kv_norm_g, w_uq, w_ukv)
    y = jax.nn.sigmoid(gate_a) * y_a + jax.nn.sigmoid(gate_b) * y_b
    x = x + g_m * (y @ w_out)
    h = rms_norm(x) * (1 + sc_f) + sh_f
    gate, up = jnp.split(h @ w_ffn_in, 2, axis=-1)
    x = x + g_f * ((jax.nn.silu(gate) * up) @ w_ffn_out)
    return x


def setup_inputs(seed: int = 0) -> dict:
    key = jax.random.key(seed)
    ks = jax.random.split(key, 24)
    f32 = jnp.float32
    x = jax.random.normal(ks[0], (BATCH, SEQ, D_MODEL), f32)
    c = jax.random.normal(ks[1], (BATCH, D_MODEL), f32)
    positions = jnp.broadcast_to(jnp.arange(SEQ, dtype=jnp.int32)[None, :], (BATCH, SEQ))
    w_ada = jax.random.normal(ks[2], (DEPTH, D_MODEL, 6 * D_MODEL), f32) * (0.3 * D_MODEL ** -0.5)
    b_ada = jax.random.normal(ks[3], (DEPTH, 6 * D_MODEL), f32) * 0.02
    w_in = jax.random.normal(ks[4], (DEPTH, D_MODEL, D_IN), f32) * D_MODEL ** -0.5
    conv_w = jax.random.normal(ks[5], (DEPTH, CONV_W, LRU_WIDTH), f32) * CONV_W ** -0.5
    conv_b = jax.random.normal(ks[6], (DEPTH, LRU_WIDTH), f32) * 0.01
    lru_wa = jax.random.normal(ks[7], (DEPTH, LRU_BLOCKS, LRU_BLOCK_W, LRU_BLOCK_W), f32) * LRU_BLOCK_W ** -0.5
    lru_ba = jax.random.normal(ks[8], (DEPTH, LRU_BLOCKS, LRU_BLOCK_W), f32) * 0.01
    lru_wx = jax.random.normal(ks[9], (DEPTH, LRU_BLOCKS, LRU_BLOCK_W, LRU_BLOCK_W), f32) * LRU_BLOCK_W ** -0.5
    lru_bx = jax.random.normal(ks[10], (DEPTH, LRU_BLOCKS, LRU_BLOCK_W), f32) * 0.01
    rad = jax.random.uniform(ks[11], (DEPTH, LRU_WIDTH), f32, minval=0.9, maxval=0.999)
    a0 = rad ** (1.0 / LRU_C)
    lru_a_param = jnp.log(a0) - jnp.log1p(-a0)
    q_norm_g = 1.0 + 0.01 * jax.random.normal(ks[12], (DEPTH, Q_RANK), f32)
    kv_norm_g = 1.0 + 0.01 * jax.random.normal(ks[13], (DEPTH, KV_RANK), f32)
    w_uq = jax.random.normal(ks[14], (DEPTH, Q_RANK, MLA_HEADS, QK_HEAD), f32) * Q_RANK ** -0.5
    w_ukv = jax.random.normal(ks[15], (DEPTH, KV_RANK, MLA_HEADS, QK_NOPE + V_HEAD), f32) * KV_RANK ** -0.5
    w_out = jax.random.normal(ks[16], (DEPTH, D_MODEL, D_MODEL), f32) * D_MODEL ** -0.5
    w_ffn_in = jax.random.normal(ks[17], (DEPTH, D_MODEL, 2 * D_FF), f32) * D_MODEL ** -0.5
    w_ffn_out = jax.random.normal(ks[18], (DEPTH, D_FF, D_MODEL), f32) * D_FF ** -0.5
    final_norm_g = 1.0 + 0.01 * jax.random.normal(ks[19], (D_MODEL,), f32)
    return {'x': x, 'c': c, 'positions': positions, 'w_ada': w_ada, 'b_ada': b_ada,
            'w_in': w_in, 'conv_w': conv_w, 'conv_b': conv_b, 'lru_wa': lru_wa,
            'lru_ba': lru_ba, 'lru_wx': lru_wx, 'lru_bx': lru_bx, 'lru_a_param': lru_a_param,
            'q_norm_g': q_norm_g, 'kv_norm_g': kv_norm_g, 'w_uq': w_uq, 'w_ukv': w_ukv,
            'w_out': w_out, 'w_ffn_in': w_ffn_in, 'w_ffn_out': w_ffn_out,
            'final_norm_g': final_norm_g}


def reference(x, c, positions, w_ada, b_ada, w_in, conv_w, conv_b, lru_wa, lru_ba, lru_wx,
              lru_bx, lru_a_param, q_norm_g, kv_norm_g, w_uq, w_ukv, w_out, w_ffn_in,
              w_ffn_out, final_norm_g):
    cos, sin = rope_tables(positions)
    for l in range(DEPTH):
        x = hybrid_layer(x, c, positions, cos, sin, w_ada[l], b_ada[l], w_in[l], conv_w[l],
                         conv_b[l], lru_wa[l], lru_ba[l], lru_wx[l], lru_bx[l], lru_a_param[l],
                         q_norm_g[l], kv_norm_g[l], w_uq[l], w_ukv[l], w_out[l],
                         w_ffn_in[l], w_ffn_out[l])
    return rms_norm(x, final_norm_g)
```

```python
import functools
import math

import jax
import jax.numpy as jnp
from jax import lax
from jax.experimental import pallas as pl
from jax.experimental.pallas import tpu as pltpu

F32 = jnp.float32
BF16 = jnp.bfloat16

D_MODEL = 1024
LRU_BLOCKS = 8
LRU_BLOCK_W = D_MODEL // LRU_BLOCKS
CONV_W = 4
LRU_C = 8.0
HEADS = 8
QK_NOPE = 128
QK_ROPE = 64
QK_HEAD = QK_NOPE + QK_ROPE
V_HEAD = D_MODEL // HEADS
Q_RANK = 256
KV_RANK = 128
ROPE_THETA = 10000.0
D_FF = -(-8 * D_MODEL // (3 * 256)) * 256
EPS = 1e-6

LANES = 128
SUBLANES = 8
HEAD_SLOT = 2 * LANES
Q_SCALE = (QK_HEAD ** -0.5) * math.log2(math.e)
VMEM_LIMIT = 56 * 1024 * 1024

C_LRU, C_GA, C_GB, C_QD, C_KV = 0, D_MODEL, 2 * D_MODEL, 3 * D_MODEL, 3 * D_MODEL + Q_RANK
D_IN_PACKED = C_KV + HEAD_SLOT


def _rms(x):
    return x * lax.rsqrt(jnp.mean(x * x, axis=-1, keepdims=True) + EPS)


def _const_spec(shape):
    nd = len(shape)
    return pl.BlockSpec(shape, lambda *_: (0,) * nd, pipeline_mode=pl.Buffered(1))


def _mod_kernel(c_ref, w_ref, b_ref, o_ref):
    o_ref[0] = jnp.dot(c_ref[...], w_ref[0], preferred_element_type=F32,
                       precision=lax.Precision.HIGHEST) + b_ref[0]


def _adaln_mod(c_pad, w_ada, b_ada):
    depth, _, n = w_ada.shape
    tn = 1536
    return pl.pallas_call(
        _mod_kernel,
        grid=(depth, n // tn),
        in_specs=[pl.BlockSpec(c_pad.shape, lambda l, j: (0, 0)),
                  pl.BlockSpec((1, D_MODEL, tn), lambda l, j: (l, 0, j)),
                  pl.BlockSpec((1, 1, tn), lambda l, j: (l, 0, j))],
        out_specs=pl.BlockSpec((1, c_pad.shape[0], tn), lambda l, j: (l, 0, j)),
        out_shape=jax.ShapeDtypeStruct((depth, c_pad.shape[0], n), F32),
        compiler_params=pltpu.CompilerParams(dimension_semantics=("arbitrary", "arbitrary")),
        name="adaln_mod",
    )(c_pad, w_ada, b_ada.reshape(depth, 1, n))


def _absorb_kernel(wq_ref, wukt_ref, o_ref):
    o_ref[0, 0] = jnp.dot(wq_ref[0, 0], wukt_ref[0, 0], preferred_element_type=F32,
                          precision=lax.Precision.HIGHEST)


def _absorb_q(wq_nope, wuk_t):
    depth = wq_nope.shape[0]
    return pl.pallas_call(
        _absorb_kernel,
        grid=(depth, HEADS),
        in_specs=[pl.BlockSpec((1, 1, Q_RANK, QK_NOPE), lambda l, h: (l, h, 0, 0)),
                  pl.BlockSpec((1, 1, QK_NOPE, KV_RANK), lambda l, h: (l, h, 0, 0))],
        out_specs=pl.BlockSpec((1, 1, Q_RANK, KV_RANK), lambda l, h: (l, h, 0, 0)),
        out_shape=jax.ShapeDtypeStruct((depth, HEADS, Q_RANK, KV_RANK), F32),
        compiler_params=pltpu.CompilerParams(dimension_semantics=("arbitrary", "arbitrary")),
        name="absorb_q",
    )(wq_nope, wuk_t)


def _rope_kernel(pos_ref, f_ref, cos_ref, sin_ref):
    ang = pos_ref[...] * f_ref[...]
    cos_ref[...] = jnp.cos(ang)
    sin_ref[...] = jnp.sin(ang)


def _rope_tables(pos_rep, freq_rep):
    rows = pos_rep.shape[0]
    tr = min(rows, 512)
    spec = pl.BlockSpec((tr, LANES), lambda i: (i, 0))
    return pl.pallas_call(
        _rope_kernel,
        grid=(rows // tr,),
        in_specs=[spec, pl.BlockSpec((1, LANES), lambda i: (0, 0))],
        out_specs=[spec, spec],
        out_shape=[jax.ShapeDtypeStruct(pos_rep.shape, F32)] * 2,
        compiler_params=pltpu.CompilerParams(dimension_semantics=("arbitrary",)),
        name="rope_tables",
    )(pos_rep, freq_rep)


def _mixer_in_kernel(x_ref, mod_ref, pos_ref, cs_ref, win_ref, convw_ref, convb_ref, wg_ref,
                     ba_ref, bx_ref, ap_ref, qg_ref, kvg_ref, wq_ref,
                     yag_ref, gb_ref, q_ref, kk_ref, vt_ref,
                     xe_s, a_s, b_s, hc_s, *, tm):
    @pl.when(pl.program_id(1) == 0)
    def _():
        xe_s[0:SUBLANES, :] = jnp.zeros((SUBLANES, D_MODEL), F32)
        hc_s[...] = jnp.zeros((SUBLANES, D_MODEL), F32)

    x = x_ref[0]
    mod = mod_ref[0]
    h = _rms(x) * (1.0 + mod[1:2]) + mod[0:1]
    proj = jnp.dot(h.astype(BF16), win_ref[...], preferred_element_type=F32)
    xl = proj[:, C_LRU:C_LRU + D_MODEL]
    gate_a = proj[:, C_GA:C_GA + D_MODEL]
    gb_ref[0] = proj[:, C_GB:C_GB + D_MODEL].astype(BF16)
    q_down = proj[:, C_QD:C_QD + Q_RANK]
    kv_slot = proj[:, C_KV:C_KV + HEAD_SLOT]

    cs = cs_ref[0]
    c_q = _rms(q_down) * qg_ref[...]
    qa = jnp.dot(c_q.astype(BF16), wq_ref[...], preferred_element_type=F32)
    for hh in range(HEADS):
        base = hh * HEAD_SLOT
        p = qa[:, base + LANES:base + HEAD_SLOT] * cs
        roped = p + pltpu.roll(p, LANES // 2, 1)
        q_ref[0, hh] = (jnp.concatenate([qa[:, base:base + LANES], roped], axis=1) * Q_SCALE).astype(BF16)
    c_kv = _rms(kv_slot[:, :KV_RANK]) * kvg_ref[...]
    p = kv_slot[:, KV_RANK:] * cs
    lane = lax.broadcasted_iota(jnp.int32, p.shape, 1)
    k_pe = jnp.where(lane < QK_ROPE, p + pltpu.roll(p, LANES // 2, 1), 0.0)
    kk_ref[0] = jnp.concatenate([c_kv, k_pe], axis=1).astype(BF16)
    vt_ref[0] = c_kv.T.astype(BF16)

    xe_s[SUBLANES:SUBLANES + tm, :] = xl
    cw = convw_ref[...]
    xc = convb_ref[...] + cw[3:4] * xl
    for k in range(CONV_W - 1):
        off = SUBLANES - (CONV_W - 1) + k
        xc = xc + cw[k:k + 1] * xe_s[off:off + tm, :]
    xe_s[0:SUBLANES, :] = xe_s[tm:tm + SUBLANES, :]

    reset = pos_ref[0] == 0
    xcb = xc.astype(BF16)
    row8 = lax.broadcasted_iota(jnp.int32, (tm, LRU_BLOCK_W), 0) & (SUBLANES - 1)
    neg_c_softplus = -LRU_C * jax.nn.softplus(-ap_ref[...])
    for n in range(LRU_BLOCKS):
        sl = slice(n * LRU_BLOCK_W, (n + 1) * LRU_BLOCK_W)
        g = jnp.dot(xcb[:, sl], wg_ref[n], preferred_element_type=F32)
        r = jax.nn.sigmoid(g[:, :LRU_BLOCK_W] + ba_ref[:, sl])
        i = jax.nn.sigmoid(g[:, LRU_BLOCK_W:] + bx_ref[:, sl])
        log_a = neg_c_softplus[:, sl] * r
        a = jnp.exp(log_a)
        mult = jnp.sqrt(jnp.tanh(-log_a) * (1.0 + a * a))
        a = jnp.where(reset, 0.0, a)
        mult = jnp.where(reset, 1.0, mult)
        b = xc[:, sl] * i * mult
        for d in (1, 2, 4):
            keep = row8 >= d
            b = jnp.where(keep, a * pltpu.roll(b, d, 0) + b, b)
            a = jnp.where(keep, a * pltpu.roll(a, d, 0), a)
        a_s[:, sl] = a
        b_s[:, sl] = b

    def group(gidx, h_prev):
        r0 = pl.multiple_of(gidx * SUBLANES, SUBLANES)
        h8 = b_s[pl.ds(r0, SUBLANES), :] + a_s[pl.ds(r0, SUBLANES), :] * h_prev
        b_s[pl.ds(r0, SUBLANES), :] = h8
        return jnp.broadcast_to(h8[SUBLANES - 1:SUBLANES, :], (SUBLANES, D_MODEL))

    hc_s[...] = lax.fori_loop(0, tm // SUBLANES, group, hc_s[...], unroll=8)
    yag_ref[0] = (jax.nn.sigmoid(gate_a) * b_s[...]).astype(BF16)


def _mixer_in(x, mod, pos3, cs, win, convw, convb, wg, ba, bx, ap, qg, kvg, wq, *, tm):
    B, S, _ = x.shape
    row = lambda b, s: (b, s, 0)
    in_specs = [
        pl.BlockSpec((1, tm, D_MODEL), row),
        pl.BlockSpec((1, SUBLANES, D_MODEL), lambda b, s: (b, 0, 0)),
        pl.BlockSpec((1, tm, 1), row),
        pl.BlockSpec((1, tm, LANES), row),
        _const_spec(win.shape), _const_spec(convw.shape), _const_spec(convb.shape),
        _const_spec(wg.shape), _const_spec(ba.shape), _const_spec(bx.shape), _const_spec(ap.shape),
        _const_spec(qg.shape), _const_spec(kvg.shape), _const_spec(wq.shape),
    ]
    out_specs = [
        pl.BlockSpec((1, tm, D_MODEL), row),
        pl.BlockSpec((1, tm, D_MODEL), row),
        pl.BlockSpec((1, HEADS, tm, HEAD_SLOT), lambda b, s: (b, 0, s, 0)),
        pl.BlockSpec((1, tm, HEAD_SLOT), row),
        pl.BlockSpec((1, KV_RANK, tm), lambda b, s: (b, 0, s)),
    ]
    out_shape = [
        jax.ShapeDtypeStruct((B, S, D_MODEL), BF16),
        jax.ShapeDtypeStruct((B, S, D_MODEL), BF16),
        jax.ShapeDtypeStruct((B, HEADS, S, HEAD_SLOT), BF16),
        jax.ShapeDtypeStruct((B, S, HEAD_SLOT), BF16),
        jax.ShapeDtypeStruct((B, KV_RANK, S), BF16),
    ]
    scratch = [pltpu.VMEM((tm + SUBLANES, D_MODEL), F32), pltpu.VMEM((tm, D_MODEL), F32),
               pltpu.VMEM((tm, D_MODEL), F32), pltpu.VMEM((SUBLANES, D_MODEL), F32)]
    return pl.pallas_call(
        functools.partial(_mixer_in_kernel, tm=tm),
        grid=(B, S // tm), in_specs=in_specs, out_specs=out_specs, out_shape=out_shape,
        scratch_shapes=scratch,
        compiler_params=pltpu.CompilerParams(dimension_semantics=("arbitrary", "arbitrary"),
                                             vmem_limit_bytes=VMEM_LIMIT),
        name="mixer_in",
    )(x, mod, pos3, cs, win, convw, convb, wg, ba, bx, ap, qg, kvg, wq)


def _attn_kernel(q_ref, kk_ref, vt_ref, wuv_ref, o_ref, acc_s, *, tq, tk):
    qi = pl.program_id(1)
    q = q_ref[0, 0]
    nsub = tq // tk

    def step(j, carry, masked):
        m, l = carry
        k0 = pl.multiple_of(j * tk, tk)
        k = kk_ref[0, pl.ds(k0, tk), :]
        s = lax.dot_general(k, q, (((1,), (1,)), ((), ())), preferred_element_type=F32)
        if masked:
            kidx = k0 + lax.broadcasted_iota(jnp.int32, (tk, tq), 0)
            qidx = qi * tq + lax.broadcasted_iota(jnp.int32, (tk, tq), 1)
            s = jnp.where(kidx <= qidx, s, -jnp.inf)
        m_new = jnp.maximum(m, jnp.max(s, axis=0, keepdims=True))
        alpha = jnp.exp2(m - m_new)
        p = jnp.exp2(s - m_new)
        l = alpha * l + jnp.sum(p, axis=0, keepdims=True)
        vt = vt_ref[0, :, pl.ds(k0, tk)]
        acc_s[...] = alpha * acc_s[...] + jnp.dot(vt, p.astype(BF16), preferred_element_type=F32)
        return m_new, l

    acc_s[...] = jnp.zeros(acc_s.shape, F32)
    carry = (jnp.full((1, tq), -jnp.inf, F32), jnp.zeros((1, tq), F32))
    carry = lax.fori_loop(0, qi * nsub, lambda j, c: step(j, c, False), carry)
    for dd in range(nsub):
        carry = step(qi * nsub + dd, carry, True)
    o = acc_s[...] * (1.0 / carry[1])
    o_ref[0, 0] = jnp.dot(o.T.astype(BF16), wuv_ref[0], preferred_element_type=F32).astype(BF16)


def _attention(q, kk, vt, wuv, *, tq, tk):
    B, H, S, _ = q.shape
    return pl.pallas_call(
        functools.partial(_attn_kernel, tq=tq, tk=tk),
        grid=(B, S // tq, H),
        in_specs=[pl.BlockSpec((1, 1, tq, HEAD_SLOT), lambda b, i, h: (b, h, i, 0)),
                  pl.BlockSpec((1, S, HEAD_SLOT), lambda b, i, h: (b, 0, 0)),
                  pl.BlockSpec((1, KV_RANK, S), lambda b, i, h: (b, 0, 0)),
                  pl.BlockSpec((1, KV_RANK, V_HEAD), lambda b, i, h: (h, 0, 0))],
        out_specs=pl.BlockSpec((1, 1, tq, V_HEAD), lambda b, i, h: (b, h, i, 0)),
        out_shape=jax.ShapeDtypeStruct((B, H, S, V_HEAD), BF16),
        scratch_shapes=[pltpu.VMEM((KV_RANK, tq), F32)],
        compiler_params=pltpu.CompilerParams(
            dimension_semantics=("arbitrary", "arbitrary", "arbitrary"), vmem_limit_bytes=VMEM_LIMIT),
        name="attention",
    )(q, kk, vt, wuv)


def _mixer_out_kernel(x_ref, yag_ref, gb_ref, yb_ref, mod_ref, wout_ref, wffi_ref, wffo_ref, fg_ref,
                      o_ref, *, final):
    x = x_ref[0]
    mod = mod_ref[0]
    yb = jnp.concatenate([yb_ref[0, hh] for hh in range(HEADS)], axis=1).astype(F32)
    y = yag_ref[0].astype(F32) + jax.nn.sigmoid(gb_ref[0].astype(F32)) * yb
    x = x + mod[2:3] * jnp.dot(y.astype(BF16), wout_ref[...], preferred_element_type=F32)
    h = _rms(x) * (1.0 + mod[4:5]) + mod[3:4]
    gu = jnp.dot(h.astype(BF16), wffi_ref[...], preferred_element_type=F32)
    act = jax.nn.silu(gu[:, :D_FF]) * gu[:, D_FF:]
    x = x + mod[5:6] * jnp.dot(act.astype(BF16), wffo_ref[...], preferred_element_type=F32)
    if final:
        x = _rms(x) * fg_ref[...]
    o_ref[0] = x


def _mixer_out(x, yag, gb, yb, mod, wout, wffi, wffo, fg, *, tm, final):
    B, S, _ = x.shape
    row = lambda b, s: (b, s, 0)
    return pl.pallas_call(
        functools.partial(_mixer_out_kernel, final=final),
        grid=(B, S // tm),
        in_specs=[pl.BlockSpec((1, tm, D_MODEL), row), pl.BlockSpec((1, tm, D_MODEL), row),
                  pl.BlockSpec((1, tm, D_MODEL), row),
                  pl.BlockSpec((1, HEADS, tm, V_HEAD), lambda b, s: (b, 0, s, 0)),
                  pl.BlockSpec((1, SUBLANES, D_MODEL), lambda b, s: (b, 0, 0)),
                  _const_spec(wout.shape), _const_spec(wffi.shape), _const_spec(wffo.shape),
                  _const_spec(fg.shape)],
        out_specs=pl.BlockSpec((1, tm, D_MODEL), row),
        out_shape=jax.ShapeDtypeStruct((B, S, D_MODEL), F32),
        compiler_params=pltpu.CompilerParams(dimension_semantics=("arbitrary", "arbitrary"),
                                             vmem_limit_bytes=VMEM_LIMIT),
        name="mixer_out",
    )(x, yag, gb, yb, mod, wout, wffi, wffo, fg)


def _rope_cols(w):
    half = QK_ROPE // 2
    x1, x2 = w[..., :half], w[..., half:]
    return jnp.concatenate([x1, x2, x2, x1], axis=-1)


def kernel(x, c, positions, w_ada, b_ada, w_in, conv_w, conv_b, lru_wa, lru_ba, lru_wx, lru_bx,
           lru_a_param, q_norm_g, kv_norm_g, w_uq, w_ukv, w_out, w_ffn_in, w_ffn_out, final_norm_g):
    B, S, _ = x.shape
    depth = w_ada.shape[0]
    tm = min(S, 512)
    tq = tk = min(S, 512)

    c_pad = jnp.zeros((SUBLANES, D_MODEL), F32).at[:B].set(c)
    mod = _adaln_mod(c_pad, w_ada, b_ada)[:, :B].reshape(depth, B, 6, D_MODEL)
    mod = jnp.pad(mod, ((0, 0), (0, 0), (0, SUBLANES - 6), (0, 0)))

    inv_freq = ROPE_THETA ** (-jnp.arange(0, QK_ROPE, 2, dtype=F32) / QK_ROPE)
    rep = LANES // inv_freq.shape[0]
    pos_rep = jnp.repeat(positions.astype(F32).reshape(-1, rep), inv_freq.shape[0], axis=1)
    cos, sin = _rope_tables(pos_rep, jnp.tile(inv_freq, rep)[None, :])
    cos, sin = cos.reshape(B, S, -1), sin.reshape(B, S, -1)
    cs = jnp.concatenate([cos, cos, -sin, sin], axis=-1)
    pos3 = positions.reshape(B, S, 1)

    wq_nope = jnp.transpose(w_uq[..., :QK_NOPE], (0, 2, 1, 3))
    wuk_t = jnp.transpose(w_ukv[..., :QK_NOPE], (0, 2, 3, 1))
    wq_abs = _absorb_q(wq_nope, wuk_t)
    wq_rope = _rope_cols(jnp.transpose(w_uq[..., QK_NOPE:], (0, 2, 1, 3)))
    wq_all = jnp.concatenate([wq_abs, wq_rope], axis=-1)
    wq_all = jnp.transpose(wq_all, (0, 2, 1, 3)).reshape(depth, Q_RANK, HEADS * HEAD_SLOT).astype(BF16)
    wuv = jnp.transpose(w_ukv[..., QK_NOPE:], (0, 2, 1, 3)).astype(BF16)

    o_q, o_kv, o_kr, o_ga, o_gb = (D_MODEL, D_MODEL + Q_RANK, D_MODEL + Q_RANK + KV_RANK,
                                   D_MODEL + Q_RANK + KV_RANK + QK_ROPE,
                                   2 * D_MODEL + Q_RANK + KV_RANK + QK_ROPE)
    win = jnp.concatenate([w_in[..., :o_q], w_in[..., o_ga:o_gb], w_in[..., o_gb:],
                           w_in[..., o_q:o_kv], w_in[..., o_kv:o_kr],
                           _rope_cols(w_in[..., o_kr:o_ga])], axis=-1).astype(BF16)
    wg = jnp.concatenate([lru_wa, lru_wx], axis=-1).astype(BF16)
    convw = jnp.pad(conv_w, ((0, 0), (0, SUBLANES - CONV_W), (0, 0)))
    wout = w_out.astype(BF16)
    wffi = w_ffn_in.astype(BF16)
    wffo = w_ffn_out.astype(BF16)
    fg = final_norm_g.reshape(1, D_MODEL)

    for l in range(depth):
        yag, gb, q, kk, vt = _mixer_in(
            x, mod[l], pos3, cs, win[l], convw[l], conv_b[l].reshape(1, -1), wg[l],
            lru_ba[l].reshape(1, -1), lru_bx[l].reshape(1, -1), lru_a_param[l].reshape(1, -1),
            q_norm_g[l].reshape(1, -1), kv_norm_g[l].reshape(1, -1), wq_all[l], tm=tm)
        yb = _attention(q, kk, vt, wuv[l], tq=tq, tk=tk)
        x = _mixer_out(x, yag, gb, yb, mod[l], wout[l], wffi[l], wffo[l], fg, tm=tm,
                       final=(l == depth - 1))
    return x
```

```python
import functools
import math

import jax
import jax.numpy as jnp
from jax import lax
from jax.experimental import pallas as pl
from jax.experimental.pallas import tpu as pltpu

F32 = jnp.float32
BF16 = jnp.bfloat16

D_MODEL = 1024
LRU_BLOCKS = 8
LRU_BLOCK_W = D_MODEL // LRU_BLOCKS
CONV_W = 4
LRU_C = 8.0
HEADS = 8
QK_NOPE = 128
QK_ROPE = 64
QK_HEAD = QK_NOPE + QK_ROPE
V_HEAD = D_MODEL // HEADS
Q_RANK = 256
KV_RANK = 128
ROPE_THETA = 10000.0
D_FF = -(-8 * D_MODEL // (3 * 256)) * 256
EPS = 1e-6

LANES = 128
SUBLANES = 8
HEAD_SLOT = 2 * LANES
BF16_ROWS = 2 * SUBLANES
V_ROWS = KV_RANK + BF16_ROWS
Q_SCALE = (QK_HEAD ** -0.5) * math.log2(math.e)
VMEM_LIMIT = 56 * 1024 * 1024

C_LRU, C_GA, C_GB, C_QD, C_KV = 0, D_MODEL, 2 * D_MODEL, 3 * D_MODEL, 3 * D_MODEL + Q_RANK
D_IN_PACKED = C_KV + HEAD_SLOT


def _rms(x):
    return x * lax.rsqrt(jnp.mean(x * x, axis=-1, keepdims=True) + EPS)


def _sigmoid(x):
    return 0.5 * jnp.tanh(0.5 * x) + 0.5


def _silu(x):
    return x * _sigmoid(x)


def _const_spec(shape):
    nd = len(shape)
    return pl.BlockSpec(shape, lambda *_: (0,) * nd, pipeline_mode=pl.Buffered(1))


def _mod_kernel(c_ref, w_ref, b_ref, o_ref):
    o_ref[0] = jnp.dot(c_ref[...], w_ref[0], preferred_element_type=F32,
                       precision=lax.Precision.HIGHEST) + b_ref[0]


def _adaln_mod(c_pad, w_ada, b_ada):
    depth, _, n = w_ada.shape
    tn = 1536
    return pl.pallas_call(
        _mod_kernel,
        grid=(depth, n // tn),
        in_specs=[pl.BlockSpec(c_pad.shape, lambda l, j: (0, 0)),
                  pl.BlockSpec((1, D_MODEL, tn), lambda l, j: (l, 0, j)),
                  pl.BlockSpec((1, 1, tn), lambda l, j: (l, 0, j))],
        out_specs=pl.BlockSpec((1, c_pad.shape[0], tn), lambda l, j: (l, 0, j)),
        out_shape=jax.ShapeDtypeStruct((depth, c_pad.shape[0], n), F32),
        compiler_params=pltpu.CompilerParams(dimension_semantics=("arbitrary", "arbitrary")),
        name="adaln_mod",
    )(c_pad, w_ada, b_ada.reshape(depth, 1, n))


def _absorb_kernel(wq_ref, wukt_ref, o_ref):
    o_ref[0, 0] = jnp.dot(wq_ref[0, 0], wukt_ref[0, 0], preferred_element_type=F32,
                          precision=lax.Precision.HIGHEST)


def _absorb_q(wq_nope, wuk_t):
    depth = wq_nope.shape[0]
    return pl.pallas_call(
        _absorb_kernel,
        grid=(depth, HEADS),
        in_specs=[pl.BlockSpec((1, 1, Q_RANK, QK_NOPE), lambda l, h: (l, h, 0, 0)),
                  pl.BlockSpec((1, 1, QK_NOPE, KV_RANK), lambda l, h: (l, h, 0, 0))],
        out_specs=pl.BlockSpec((1, 1, Q_RANK, KV_RANK), lambda l, h: (l, h, 0, 0)),
        out_shape=jax.ShapeDtypeStruct((depth, HEADS, Q_RANK, KV_RANK), F32),
        compiler_params=pltpu.CompilerParams(dimension_semantics=("arbitrary", "arbitrary")),
        name="absorb_q",
    )(wq_nope, wuk_t)


def _rope_kernel(pos_ref, f_ref, cos_ref, sin_ref):
    ang = pos_ref[...] * f_ref[...]
    cos_ref[...] = jnp.cos(ang)
    sin_ref[...] = jnp.sin(ang)


def _rope_tables(pos_rep, freq_rep):
    rows = pos_rep.shape[0]
    tr = min(rows, 512)
    spec = pl.BlockSpec((tr, LANES), lambda i: (i, 0))
    return pl.pallas_call(
        _rope_kernel,
        grid=(rows // tr,),
        in_specs=[spec, pl.BlockSpec((1, LANES), lambda i: (0, 0))],
        out_specs=[spec, spec],
        out_shape=[jax.ShapeDtypeStruct(pos_rep.shape, F32)] * 2,
        compiler_params=pltpu.CompilerParams(dimension_semantics=("arbitrary",)),
        name="rope_tables",
    )(pos_rep, freq_rep)


def _mixer_in_kernel(x_ref, mod_ref, pos_ref, cs_ref, win_ref, convw_ref, convb_ref, wg_ref,
                     ba_ref, bx_ref, ap_ref, qg_ref, kvg_ref, wq_ref,
                     yag_ref, gb_ref, q_ref, kk_ref, vt_ref,
                     xe_s, a_s, b_s, hc_s, *, tm):
    @pl.when(pl.program_id(1) == 0)
    def _():
        xe_s[0:SUBLANES, :] = jnp.zeros((SUBLANES, D_MODEL), F32)
        hc_s[...] = jnp.zeros((SUBLANES, D_MODEL), F32)

    x = x_ref[0]
    mod = mod_ref[0]
    h = _rms(x) * (1.0 + mod[1:2]) + mod[0:1]
    proj = jnp.dot(h.astype(BF16), win_ref[...], preferred_element_type=F32)
    xl = proj[:, C_LRU:C_LRU + D_MODEL]
    gate_a = proj[:, C_GA:C_GA + D_MODEL]
    gb_ref[0] = proj[:, C_GB:C_GB + D_MODEL].astype(BF16)
    q_down = proj[:, C_QD:C_QD + Q_RANK]
    kv_slot = proj[:, C_KV:C_KV + HEAD_SLOT]

    cs = cs_ref[0]
    c_q = _rms(q_down) * qg_ref[...]
    qa = jnp.dot(c_q.astype(BF16), wq_ref[...], preferred_element_type=F32)
    for hh in range(HEADS):
        base = hh * HEAD_SLOT
        p = qa[:, base + LANES:base + HEAD_SLOT] * cs
        roped = p + pltpu.roll(p, LANES // 2, 1)
        q_ref[0, hh] = (jnp.concatenate([qa[:, base:base + LANES], roped], axis=1) * Q_SCALE).astype(BF16)
    c_kv = _rms(kv_slot[:, :KV_RANK]) * kvg_ref[...]
    p = kv_slot[:, KV_RANK:] * cs
    lane = lax.broadcasted_iota(jnp.int32, p.shape, 1)
    k_pe = jnp.where(lane < QK_ROPE, p + pltpu.roll(p, LANES // 2, 1), 0.0)
    kk_ref[0] = jnp.concatenate([c_kv, k_pe], axis=1).astype(BF16)
    vt_ref[0, 0:KV_RANK, :] = c_kv.T.astype(BF16)
    ones_row = lax.broadcasted_iota(jnp.int32, (BF16_ROWS, tm), 0) == 0
    vt_ref[0, KV_RANK:V_ROWS, :] = jnp.where(ones_row, 1.0, 0.0).astype(BF16)

    xe_s[SUBLANES:SUBLANES + tm, :] = xl
    cw = convw_ref[...]
    xc = convb_ref[...] + cw[3:4] * xl
    for k in range(CONV_W - 1):
        off = SUBLANES - (CONV_W - 1) + k
        xc = xc + cw[k:k + 1] * xe_s[off:off + tm, :]
    xe_s[0:SUBLANES, :] = xe_s[tm:tm + SUBLANES, :]

    reset = pos_ref[0] == 0
    xcb = xc.astype(BF16)
    row8 = lax.broadcasted_iota(jnp.int32, (tm, LRU_BLOCK_W), 0) & (SUBLANES - 1)
    neg_c_softplus = -LRU_C * jax.nn.softplus(-ap_ref[...])
    for n in range(LRU_BLOCKS):
        sl = slice(n * LRU_BLOCK_W, (n + 1) * LRU_BLOCK_W)
        g = jnp.dot(xcb[:, sl], wg_ref[n], preferred_element_type=F32)
        r = _sigmoid(g[:, :LRU_BLOCK_W] + ba_ref[:, sl])
        i = _sigmoid(g[:, LRU_BLOCK_W:] + bx_ref[:, sl])
        log_a = neg_c_softplus[:, sl] * r
        a = jnp.exp(log_a)
        mult = jnp.sqrt(jnp.tanh(-log_a) * (1.0 + a * a))
        a = jnp.where(reset, 0.0, a)
        mult = jnp.where(reset, 1.0, mult)
        b = xc[:, sl] * i * mult
        for d in (1, 2, 4):
            keep = row8 >= d
            b = jnp.where(keep, a * pltpu.roll(b, d, 0) + b, b)
            a = jnp.where(keep, a * pltpu.roll(a, d, 0), a)
        a_s[:, sl] = a
        b_s[:, sl] = b

    def group(gidx, h_prev):
        r0 = pl.multiple_of(gidx * SUBLANES, SUBLANES)
        h8 = b_s[pl.ds(r0, SUBLANES), :] + a_s[pl.ds(r0, SUBLANES), :] * h_prev
        b_s[pl.ds(r0, SUBLANES), :] = h8
        return jnp.broadcast_to(h8[SUBLANES - 1:SUBLANES, :], (SUBLANES, D_MODEL))

    hc_s[...] = lax.fori_loop(0, tm // SUBLANES, group, hc_s[...], unroll=8)
    yag_ref[0] = (_sigmoid(gate_a) * b_s[...]).astype(BF16)


def _mixer_in(x, mod, pos3, cs, win, convw, convb, wg, ba, bx, ap, qg, kvg, wq, *, tm):
    B, S, _ = x.shape
    row = lambda b, s: (b, s, 0)
    in_specs = [
        pl.BlockSpec((1, tm, D_MODEL), row),
        pl.BlockSpec((1, SUBLANES, D_MODEL), lambda b, s: (b, 0, 0)),
        pl.BlockSpec((1, tm, 1), row),
        pl.BlockSpec((1, tm, LANES), row),
        _const_spec(win.shape), _const_spec(convw.shape), _const_spec(convb.shape),
        _const_spec(wg.shape), _const_spec(ba.shape), _const_spec(bx.shape), _const_spec(ap.shape),
        _const_spec(qg.shape), _const_spec(kvg.shape), _const_spec(wq.shape),
    ]
    out_specs = [
        pl.BlockSpec((1, tm, D_MODEL), row),
        pl.BlockSpec((1, tm, D_MODEL), row),
        pl.BlockSpec((1, HEADS, tm, HEAD_SLOT), lambda b, s: (b, 0, s, 0)),
        pl.BlockSpec((1, tm, HEAD_SLOT), row),
        pl.BlockSpec((1, V_ROWS, tm), lambda b, s: (b, 0, s)),
    ]
    out_shape = [
        jax.ShapeDtypeStruct((B, S, D_MODEL), BF16),
        jax.ShapeDtypeStruct((B, S, D_MODEL), BF16),
        jax.ShapeDtypeStruct((B, HEADS, S, HEAD_SLOT), BF16),
        jax.ShapeDtypeStruct((B, S, HEAD_SLOT), BF16),
        jax.ShapeDtypeStruct((B, V_ROWS, S), BF16),
    ]
    scratch = [pltpu.VMEM((tm + SUBLANES, D_MODEL), F32), pltpu.VMEM((tm, D_MODEL), F32),
               pltpu.VMEM((tm, D_MODEL), F32), pltpu.VMEM((SUBLANES, D_MODEL), F32)]
    return pl.pallas_call(
        functools.partial(_mixer_in_kernel, tm=tm),
        grid=(B, S // tm), in_specs=in_specs, out_specs=out_specs, out_shape=out_shape,
        scratch_shapes=scratch,
        compiler_params=pltpu.CompilerParams(dimension_semantics=("arbitrary", "arbitrary"),
                                             vmem_limit_bytes=VMEM_LIMIT),
        name="mixer_in",
    )(x, mod, pos3, cs, win, convw, convb, wg, ba, bx, ap, qg, kvg, wq)


def _attn_kernel(q_ref, kk_ref, vt_ref, wuv_ref, o_ref, sa_s, sb_s, m_s, acc_s, *, tq):
    qi = pl.program_id(1)
    bufs = (sa_s, sb_s)

    def scores(j, h):
        k0 = pl.multiple_of(j * tq, tq)
        return lax.dot_general(kk_ref[0, pl.ds(k0, tq), :], q_ref[0, h], (((1,), (1,)), ((), ())),
                               preferred_element_type=F32)

    def accumulate(j, h, s_ref, masked):
        k0 = pl.multiple_of(j * tq, tq)
        s = s_ref[...]
        if masked:
            kidx = lax.broadcasted_iota(jnp.int32, (tq, tq), 0)
            qidx = lax.broadcasted_iota(jnp.int32, (tq, tq), 1)
            s = jnp.where(kidx <= qidx, s, -jnp.inf)
        m = m_s[h]
        m_new = jnp.maximum(m, jnp.max(s, axis=0, keepdims=True))
        m_s[h] = m_new
        alpha = jnp.exp2(m - m_new)
        p = jnp.exp2((s - m_new).astype(BF16))
        vt = vt_ref[0, :, pl.ds(k0, tq)]
        acc_s[h] = alpha * acc_s[h] + jnp.dot(vt, p, preferred_element_type=F32)

    def finalize(h):
        o = acc_s[h, 0:KV_RANK, :] * (1.0 / acc_s[h, KV_RANK:KV_RANK + 1, :])
        o_ref[0, h] = jnp.dot(o.T.astype(BF16), wuv_ref[h], preferred_element_type=F32).astype(BF16)

    def key_block(j, last):
        for h in range(HEADS):
            if h + 1 < HEADS:
                bufs[(h + 1) % 2][...] = scores(j, h + 1)
            elif not last:
                bufs[(h + 1) % 2][...] = scores(j + 1, 0)
            accumulate(j, h, bufs[h % 2], last)
            if last:
                finalize(h)

    acc_s[...] = jnp.zeros(acc_s.shape, F32)
    m_s[...] = jnp.full(m_s.shape, -jnp.inf, F32)
    sa_s[...] = scores(0, 0)

    def trip(j, c):
        key_block(j, False)
        return c

    lax.fori_loop(0, qi, trip, 0)
    key_block(qi, True)


def _attention(q, kk, vt, wuv, *, tq):
    B, H, S, _ = q.shape
    return pl.pallas_call(
        functools.partial(_attn_kernel, tq=tq),
        grid=(B, S // tq),
        in_specs=[pl.BlockSpec((1, H, tq, HEAD_SLOT), lambda b, i: (b, 0, i, 0)),
                  pl.BlockSpec((1, S, HEAD_SLOT), lambda b, i: (b, 0, 0)),
                  pl.BlockSpec((1, V_ROWS, S), lambda b, i: (b, 0, 0)),
                  _const_spec(wuv.shape)],
        out_specs=pl.BlockSpec((1, H, tq, V_HEAD), lambda b, i: (b, 0, i, 0)),
        out_shape=jax.ShapeDtypeStruct((B, H, S, V_HEAD), BF16),
        scratch_shapes=[pltpu.VMEM((tq, tq), F32), pltpu.VMEM((tq, tq), F32),
                        pltpu.VMEM((H, 1, tq), F32), pltpu.VMEM((H, V_ROWS, tq), F32)],
        compiler_params=pltpu.CompilerParams(
            dimension_semantics=("arbitrary", "arbitrary"), vmem_limit_bytes=VMEM_LIMIT),
        name="attention",
    )(q, kk, vt, wuv)


def _mixer_out_kernel(x_ref, yag_ref, gb_ref, yb_ref, mod_ref, wout_ref, wffi_ref, wffo_ref, fg_ref,
                      o_ref, *, final):
    x = x_ref[0]
    mod = mod_ref[0]
    yb = jnp.concatenate([yb_ref[0, hh] for hh in range(HEADS)], axis=1).astype(F32)
    y = yag_ref[0].astype(F32) + _sigmoid(gb_ref[0].astype(F32)) * yb
    x = x + mod[2:3] * jnp.dot(y.astype(BF16), wout_ref[...], preferred_element_type=F32)
    h = _rms(x) * (1.0 + mod[4:5]) + mod[3:4]
    gu = jnp.dot(h.astype(BF16), wffi_ref[...], preferred_element_type=F32)
    act = _silu(gu[:, :D_FF]) * gu[:, D_FF:]
    x = x + mod[5:6] * jnp.dot(act.astype(BF16), wffo_ref[...], preferred_element_type=F32)
    if final:
        x = _rms(x) * fg_ref[...]
    o_ref[0] = x


def _mixer_out(x, yag, gb, yb, mod, wout, wffi, wffo, fg, *, tm, final):
    B, S, _ = x.shape
    row = lambda b, s: (b, s, 0)
    return pl.pallas_call(
        functools.partial(_mixer_out_kernel, final=final),
        grid=(B, S // tm),
        in_specs=[pl.BlockSpec((1, tm, D_MODEL), row), pl.BlockSpec((1, tm, D_MODEL), row),
                  pl.BlockSpec((1, tm, D_MODEL), row),
                  pl.BlockSpec((1, HEADS, tm, V_HEAD), lambda b, s: (b, 0, s, 0)),
                  pl.BlockSpec((1, SUBLANES, D_MODEL), lambda b, s: (b, 0, 0)),
                  _const_spec(wout.shape), _const_spec(wffi.shape), _const_spec(wffo.shape),
                  _const_spec(fg.shape)],
        out_specs=pl.BlockSpec((1, tm, D_MODEL), row),
        out_shape=jax.ShapeDtypeStruct((B, S, D_MODEL), F32),
        compiler_params=pltpu.CompilerParams(dimension_semantics=("arbitrary", "arbitrary"),
                                             vmem_limit_bytes=VMEM_LIMIT),
        name="mixer_out",
    )(x, yag, gb, yb, mod, wout, wffi, wffo, fg)


def _rope_cols(w):
    half = QK_ROPE // 2
    x1, x2 = w[..., :half], w[..., half:]
    return jnp.concatenate([x1, x2, x2, x1], axis=-1)


def kernel(x, c, positions, w_ada, b_ada, w_in, conv_w, conv_b, lru_wa, lru_ba, lru_wx, lru_bx,
           lru_a_param, q_norm_g, kv_norm_g, w_uq, w_ukv, w_out, w_ffn_in, w_ffn_out, final_norm_g):
    B, S, _ = x.shape
    depth = w_ada.shape[0]
    tm = min(S, 512)
    tq = min(S, 512)

    c_pad = jnp.zeros((SUBLANES, D_MODEL), F32).at[:B].set(c)
    mod = _adaln_mod(c_pad, w_ada, b_ada)[:, :B].reshape(depth, B, 6, D_MODEL)
    mod = jnp.pad(mod, ((0, 0), (0, 0), (0, SUBLANES - 6), (0, 0)))

    inv_freq = ROPE_THETA ** (-jnp.arange(0, QK_ROPE, 2, dtype=F32) / QK_ROPE)
    rep = LANES // inv_freq.shape[0]
    pos_rep = jnp.repeat(positions.astype(F32).reshape(-1, rep), inv_freq.shape[0], axis=1)
    cos, sin = _rope_tables(pos_rep, jnp.tile(inv_freq, rep)[None, :])
    cos, sin = cos.reshape(B, S, -1), sin.reshape(B, S, -1)
    cs = jnp.concatenate([cos, cos, -sin, sin], axis=-1)
    pos3 = positions.reshape(B, S, 1)

    wq_nope = jnp.transpose(w_uq[..., :QK_NOPE], (0, 2, 1, 3))
    wuk_t = jnp.transpose(w_ukv[..., :QK_NOPE], (0, 2, 3, 1))
    wq_abs = _absorb_q(wq_nope, wuk_t)
    wq_rope = _rope_cols(jnp.transpose(w_uq[..., QK_NOPE:], (0, 2, 1, 3)))
    wq_all = jnp.concatenate([wq_abs, wq_rope], axis=-1)
    wq_all = jnp.transpose(wq_all, (0, 2, 1, 3)).reshape(depth, Q_RANK, HEADS * HEAD_SLOT).astype(BF16)
    wuv = jnp.transpose(w_ukv[..., QK_NOPE:], (0, 2, 1, 3)).astype(BF16)

    o_q, o_kv, o_kr, o_ga, o_gb = (D_MODEL, D_MODEL + Q_RANK, D_MODEL + Q_RANK + KV_RANK,
                                   D_MODEL + Q_RANK + KV_RANK + QK_ROPE,
                                   2 * D_MODEL + Q_RANK + KV_RANK + QK_ROPE)
    win = jnp.concatenate([w_in[..., :o_q], w_in[..., o_ga:o_gb], w_in[..., o_gb:],
                           w_in[..., o_q:o_kv], w_in[..., o_kv:o_kr],
                           _rope_cols(w_in[..., o_kr:o_ga])], axis=-1).astype(BF16)
    wg = jnp.concatenate([lru_wa, lru_wx], axis=-1).astype(BF16)
    convw = jnp.pad(conv_w, ((0, 0), (0, SUBLANES - CONV_W), (0, 0)))
    wout = w_out.astype(BF16)
    wffi = w_ffn_in.astype(BF16)
    wffo = w_ffn_out.astype(BF16)
    fg = final_norm_g.reshape(1, D_MODEL)

    for l in range(depth):
        yag, gb, q, kk, vt = _mixer_in(
            x, mod[l], pos3, cs, win[l], convw[l], conv_b[l].reshape(1, -1), wg[l],
            lru_ba[l].reshape(1, -1), lru_bx[l].reshape(1, -1), lru_a_param[l].reshape(1, -1),
            q_norm_g[l].reshape(1, -1), kv_norm_g[l].reshape(1, -1), wq_all[l], tm=tm)
        yb = _attention(q, kk, vt, wuv[l], tq=tq)
        x = _mixer_out(x, yag, gb, yb, mod[l], wout[l], wffi[l], wffo[l], fg, tm=tm,
                       final=(l == depth - 1))
    return x
```

```python
import functools
import math

import jax
import jax.numpy as jnp
from jax import lax
from jax.experimental import pallas as pl
from jax.experimental.pallas import tpu as pltpu

F32 = jnp.float32
BF16 = jnp.bfloat16

D_MODEL = 1024
LRU_BLOCKS = 8
LRU_BLOCK_W = D_MODEL // LRU_BLOCKS
CONV_W = 4
LRU_C = 8.0
HEADS = 8
QK_NOPE = 128
QK_ROPE = 64
QK_HEAD = QK_NOPE + QK_ROPE
V_HEAD = D_MODEL // HEADS
Q_RANK = 256
KV_RANK = 128
ROPE_THETA = 10000.0
D_FF = -(-8 * D_MODEL // (3 * 256)) * 256
EPS = 1e-6

LANES = 128
SUBLANES = 8
HEAD_SLOT = 2 * LANES
BF16_ROWS = 2 * SUBLANES
V_ROWS = KV_RANK + BF16_ROWS
Q_SCALE = (QK_HEAD ** -0.5) * math.log2(math.e)
VMEM_LIMIT = 56 * 1024 * 1024

C_LRU, C_GA, C_GB, C_QD, C_KV = 0, D_MODEL, 2 * D_MODEL, 3 * D_MODEL, 3 * D_MODEL + Q_RANK
D_IN_PACKED = C_KV + HEAD_SLOT


def _rms(x):
    return x * lax.rsqrt(jnp.mean(x * x, axis=-1, keepdims=True) + EPS)


def _sigmoid(x):
    return 0.5 * jnp.tanh(0.5 * x) + 0.5


def _silu(x):
    return x * _sigmoid(x)


def _const_spec(shape):
    nd = len(shape)
    return pl.BlockSpec(shape, lambda *_: (0,) * nd, pipeline_mode=pl.Buffered(1))


def _mod_kernel(c_ref, w_ref, b_ref, o_ref):
    o_ref[0] = jnp.dot(c_ref[...], w_ref[0], preferred_element_type=F32,
                       precision=lax.Precision.HIGHEST) + b_ref[0]


def _adaln_mod(c_pad, w_ada, b_ada):
    depth, _, n = w_ada.shape
    tn = 1536
    return pl.pallas_call(
        _mod_kernel,
        grid=(depth, n // tn),
        in_specs=[pl.BlockSpec(c_pad.shape, lambda l, j: (0, 0)),
                  pl.BlockSpec((1, D_MODEL, tn), lambda l, j: (l, 0, j)),
                  pl.BlockSpec((1, 1, tn), lambda l, j: (l, 0, j))],
        out_specs=pl.BlockSpec((1, c_pad.shape[0], tn), lambda l, j: (l, 0, j)),
        out_shape=jax.ShapeDtypeStruct((depth, c_pad.shape[0], n), F32),
        compiler_params=pltpu.CompilerParams(dimension_semantics=("arbitrary", "arbitrary")),
        name="adaln_mod",
    )(c_pad, w_ada, b_ada.reshape(depth, 1, n))


def _absorb_kernel(wq_ref, wukt_ref, o_ref):
    o_ref[0, 0] = jnp.dot(wq_ref[0, 0], wukt_ref[0, 0], preferred_element_type=F32,
                          precision=lax.Precision.HIGHEST)


def _absorb_q(wq_nope, wuk_t):
    depth = wq_nope.shape[0]
    return pl.pallas_call(
        _absorb_kernel,
        grid=(depth, HEADS),
        in_specs=[pl.BlockSpec((1, 1, Q_RANK, QK_NOPE), lambda l, h: (l, h, 0, 0)),
                  pl.BlockSpec((1, 1, QK_NOPE, KV_RANK), lambda l, h: (l, h, 0, 0))],
        out_specs=pl.BlockSpec((1, 1, Q_RANK, KV_RANK), lambda l, h: (l, h, 0, 0)),
        out_shape=jax.ShapeDtypeStruct((depth, HEADS, Q_RANK, KV_RANK), F32),
        compiler_params=pltpu.CompilerParams(dimension_semantics=("arbitrary", "arbitrary")),
        name="absorb_q",
    )(wq_nope, wuk_t)


def _rope_kernel(pos_ref, f_ref, cos_ref, sin_ref):
    ang = pos_ref[...] * f_ref[...]
    cos_ref[...] = jnp.cos(ang)
    sin_ref[...] = jnp.sin(ang)


def _rope_tables(pos_rep, freq_rep):
    rows = pos_rep.shape[0]
    tr = min(rows, 512)
    spec = pl.BlockSpec((tr, LANES), lambda i: (i, 0))
    return pl.pallas_call(
        _rope_kernel,
        grid=(rows // tr,),
        in_specs=[spec, pl.BlockSpec((1, LANES), lambda i: (0, 0))],
        out_specs=[spec, spec],
        out_shape=[jax.ShapeDtypeStruct(pos_rep.shape, F32)] * 2,
        compiler_params=pltpu.CompilerParams(dimension_semantics=("arbitrary",)),
        name="rope_tables",
    )(pos_rep, freq_rep)


def _mixer_in_kernel(x_ref, mod_ref, pos_ref, cs_ref, win_ref, convw_ref, convb_ref, wg_ref,
                     ba_ref, bx_ref, ap_ref, qg_ref, kvg_ref, wq_ref,
                     yag_ref, gb_ref, q_ref, kk_ref, vt_ref,
                     xe_s, a_s, b_s, hc_s, *, tm):
    @pl.when(pl.program_id(1) == 0)
    def _():
        xe_s[...] = jnp.zeros((SUBLANES, D_MODEL), F32)
        hc_s[...] = jnp.zeros((SUBLANES, D_MODEL), F32)

    x = x_ref[0]
    mod = mod_ref[0]
    h = _rms(x) * (1.0 + mod[1:2]) + mod[0:1]
    proj = jnp.dot(h.astype(BF16), win_ref[...], preferred_element_type=F32)
    xl = proj[:, C_LRU:C_LRU + D_MODEL]
    gate_a = proj[:, C_GA:C_GA + D_MODEL]
    gb_ref[0] = proj[:, C_GB:C_GB + D_MODEL].astype(BF16)
    q_down = proj[:, C_QD:C_QD + Q_RANK]
    kv_slot = proj[:, C_KV:C_KV + HEAD_SLOT]

    cs = cs_ref[0]
    c_q = _rms(q_down) * (qg_ref[...] * Q_SCALE)
    qa = jnp.dot(c_q.astype(BF16), wq_ref[...], preferred_element_type=F32)
    for hh in range(HEADS):
        base = hh * HEAD_SLOT
        p = qa[:, base + LANES:base + HEAD_SLOT] * cs
        roped = p + pltpu.roll(p, LANES // 2, 1)
        q_ref[0, hh] = jnp.concatenate([qa[:, base:base + LANES], roped], axis=1).astype(BF16)
    c_kv = _rms(kv_slot[:, :KV_RANK]) * kvg_ref[...]
    p = kv_slot[:, KV_RANK:] * cs
    lane = lax.broadcasted_iota(jnp.int32, p.shape, 1)
    k_pe = jnp.where(lane < QK_ROPE, p + pltpu.roll(p, LANES // 2, 1), 0.0)
    kk_ref[0] = jnp.concatenate([c_kv, k_pe], axis=1).astype(BF16)
    vt_ref[0, 0:KV_RANK, :] = c_kv.T.astype(BF16)
    ones_row = lax.broadcasted_iota(jnp.int32, (BF16_ROWS, tm), 0) == 0
    vt_ref[0, KV_RANK:V_ROWS, :] = jnp.where(ones_row, 1.0, 0.0).astype(BF16)

    groups = tm // SUBLANES
    x3 = xl.reshape(groups, SUBLANES, D_MODEL)
    row_d = lax.broadcasted_iota(jnp.int32, (groups, SUBLANES, D_MODEL), 1)
    cw = convw_ref[...]
    xc3 = convb_ref[...] + cw[CONV_W - 1:CONV_W] * x3
    for k in range(1, CONV_W):
        rot = pltpu.roll(x3, k, 1)
        rot_prev = jnp.concatenate([pltpu.roll(xe_s[...][None], k, 1), rot[:-1]], axis=0)
        xc3 = xc3 + cw[CONV_W - 1 - k:CONV_W - k] * jnp.where(row_d < k, rot_prev, rot)
    xe_s[...] = x3[groups - 1]
    xc = xc3.reshape(tm, D_MODEL)

    reset = pos_ref[0] == 0
    xcb = xc.astype(BF16)
    row8 = lax.broadcasted_iota(jnp.int32, (groups, SUBLANES, LRU_BLOCK_W), 1)
    c_softplus = LRU_C * jax.nn.softplus(-ap_ref[...])
    for n in range(LRU_BLOCKS):
        sl = slice(n * LRU_BLOCK_W, (n + 1) * LRU_BLOCK_W)
        g = jnp.dot(xcb[:, sl], wg_ref[n], preferred_element_type=F32)
        r = _sigmoid(g[:, :LRU_BLOCK_W] + ba_ref[:, sl])
        i = _sigmoid(g[:, LRU_BLOCK_W:] + bx_ref[:, sl])
        neg_log_a = c_softplus[:, sl] * r
        a = jnp.exp2((-math.log2(math.e) * c_softplus[:, sl]) * r)
        mult = jnp.sqrt(jnp.tanh(neg_log_a) * (1.0 + a * a))
        xi = xc[:, sl] * i
        b = jnp.where(reset, xi, xi * mult).reshape(groups, SUBLANES, LRU_BLOCK_W)
        a = jnp.where(reset, 0.0, a).reshape(groups, SUBLANES, LRU_BLOCK_W)
        for d in (1, 2, 4):
            keep = row8 >= d
            b = jnp.where(keep, a * pltpu.roll(b, d, 1) + b, b)
            a = jnp.where(keep, a * pltpu.roll(a, d, 1), a)
        a_s[:, sl] = a.reshape(tm, LRU_BLOCK_W)
        b_s[:, sl] = b.reshape(tm, LRU_BLOCK_W)

    def group(gidx, h_prev):
        r0 = pl.multiple_of(gidx * SUBLANES, SUBLANES)
        h8 = b_s[pl.ds(r0, SUBLANES), :] + a_s[pl.ds(r0, SUBLANES), :] * h_prev
        b_s[pl.ds(r0, SUBLANES), :] = h8
        return jnp.broadcast_to(h8[SUBLANES - 1:SUBLANES, :], (SUBLANES, D_MODEL))

    hc_s[...] = lax.fori_loop(0, tm // SUBLANES, group, hc_s[...], unroll=8)
    yag_ref[0] = (_sigmoid(gate_a) * b_s[...]).astype(BF16)


def _mixer_in(x, mod, pos3, cs, win, convw, convb, wg, ba, bx, ap, qg, kvg, wq, *, tm):
    B, S, _ = x.shape
    row = lambda b, s: (b, s, 0)
    in_specs = [
        pl.BlockSpec((1, tm, D_MODEL), row),
        pl.BlockSpec((1, SUBLANES, D_MODEL), lambda b, s: (b, 0, 0)),
        pl.BlockSpec((1, tm, 1), row),
        pl.BlockSpec((1, tm, LANES), row),
        _const_spec(win.shape), _const_spec(convw.shape), _const_spec(convb.shape),
        _const_spec(wg.shape), _const_spec(ba.shape), _const_spec(bx.shape), _const_spec(ap.shape),
        _const_spec(qg.shape), _const_spec(kvg.shape), _const_spec(wq.shape),
    ]
    out_specs = [
        pl.BlockSpec((1, tm, D_MODEL), row),
        pl.BlockSpec((1, tm, D_MODEL), row),
        pl.BlockSpec((1, HEADS, tm, HEAD_SLOT), lambda b, s: (b, 0, s, 0)),
        pl.BlockSpec((1, tm, HEAD_SLOT), row),
        pl.BlockSpec((1, V_ROWS, tm), lambda b, s: (b, 0, s)),
    ]
    out_shape = [
        jax.ShapeDtypeStruct((B, S, D_MODEL), BF16),
        jax.ShapeDtypeStruct((B, S, D_MODEL), BF16),
        jax.ShapeDtypeStruct((B, HEADS, S, HEAD_SLOT), BF16),
        jax.ShapeDtypeStruct((B, S, HEAD_SLOT), BF16),
        jax.ShapeDtypeStruct((B, V_ROWS, S), BF16),
    ]
    scratch = [pltpu.VMEM((SUBLANES, D_MODEL), F32), pltpu.VMEM((tm, D_MODEL), F32),
               pltpu.VMEM((tm, D_MODEL), F32), pltpu.VMEM((SUBLANES, D_MODEL), F32)]
    return pl.pallas_call(
        functools.partial(_mixer_in_kernel, tm=tm),
        grid=(B, S // tm), in_specs=in_specs, out_specs=out_specs, out_shape=out_shape,
        scratch_shapes=scratch,
        compiler_params=pltpu.CompilerParams(dimension_semantics=("arbitrary", "arbitrary"),
                                             vmem_limit_bytes=VMEM_LIMIT),
        name="mixer_in",
    )(x, mod, pos3, cs, win, convw, convb, wg, ba, bx, ap, qg, kvg, wq)


def _attn_kernel(q_ref, kk_ref, vt_ref, wuv_ref, o_ref, sa_s, sb_s, mx_s, m_s, acc_s, *, tq):
    qi = pl.program_id(1)
    bufs = (sa_s, sb_s)

    def produce(j, h, buf, mask):
        k0 = pl.multiple_of(j * tq, tq)
        s = lax.dot_general(kk_ref[0, pl.ds(k0, tq), :], q_ref[0, h], (((1,), (1,)), ((), ())),
                            preferred_element_type=F32)
        if mask is not None:
            kidx = lax.broadcasted_iota(jnp.int32, (tq, tq), 0)
            qidx = lax.broadcasted_iota(jnp.int32, (tq, tq), 1)
            if mask == "any":
                qidx = qidx + (qi - j) * tq
            s = jnp.where(kidx <= qidx, s, -jnp.inf)
        buf[...] = s
        return jnp.max(s, axis=0, keepdims=True)

    def consume(j, h, buf, mx):
        k0 = pl.multiple_of(j * tq, tq)
        m = m_s[h]
        m_new = jnp.maximum(m, mx)
        m_s[h] = m_new
        alpha = jnp.exp2(m - m_new)
        p = jnp.exp2((buf[...] - m_new).astype(BF16))
        vt = vt_ref[0, :, pl.ds(k0, tq)]
        acc_s[h] = alpha * acc_s[h] + jnp.dot(vt, p, preferred_element_type=F32)

    def finalize(h):
        o = acc_s[h, 0:KV_RANK, :] * (1.0 / acc_s[h, KV_RANK:KV_RANK + 1, :])
        o_ref[0, h] = jnp.dot(o.T.astype(BF16), wuv_ref[h], preferred_element_type=F32).astype(BF16)

    def key_block(j, last):
        mx = mx_s[...]
        for h in range(HEADS):
            nxt = bufs[(h + 1) % 2]
            if h + 1 < HEADS:
                mx_next = produce(j, h + 1, nxt, "diag" if last else None)
            elif not last:
                mx_next = produce(j + 1, 0, nxt, "any")
                mx_s[...] = mx_next
            consume(j, h, bufs[h % 2], mx)
            mx = mx_next
            if last:
                finalize(h)

    acc_s[...] = jnp.zeros(acc_s.shape, F32)
    m_s[...] = jnp.full(m_s.shape, -jnp.inf, F32)
    mx_s[...] = produce(0, 0, sa_s, "any")

    def trip(j, c):
        key_block(j, False)
        return c

    lax.fori_loop(0, qi, trip, 0)
    key_block(qi, True)


def _attention(q, kk, vt, wuv, *, tq):
    B, H, S, _ = q.shape
    return pl.pallas_call(
        functools.partial(_attn_kernel, tq=tq),
        grid=(B, S // tq),
        in_specs=[pl.BlockSpec((1, H, tq, HEAD_SLOT), lambda b, i: (b, 0, i, 0)),
                  pl.BlockSpec((1, S, HEAD_SLOT), lambda b, i: (b, 0, 0)),
                  pl.BlockSpec((1, V_ROWS, S), lambda b, i: (b, 0, 0)),
                  _const_spec(wuv.shape)],
        out_specs=pl.BlockSpec((1, H, tq, V_HEAD), lambda b, i: (b, 0, i, 0)),
        out_shape=jax.ShapeDtypeStruct((B, H, S, V_HEAD), BF16),
        scratch_shapes=[pltpu.VMEM((tq, tq), F32), pltpu.VMEM((tq, tq), F32),
                        pltpu.VMEM((1, tq), F32), pltpu.VMEM((H, 1, tq), F32),
                        pltpu.VMEM((H, V_ROWS, tq), F32)],
        compiler_params=pltpu.CompilerParams(
            dimension_semantics=("arbitrary", "arbitrary"), vmem_limit_bytes=VMEM_LIMIT),
        name="attention",
    )(q, kk, vt, wuv)


def _mixer_out_kernel(x_ref, yag_ref, gb_ref, yb_ref, mod_ref, wout_ref, wffi_ref, wffo_ref, fg_ref,
                      o_ref, *, final):
    x = x_ref[0]
    mod = mod_ref[0]
    yb = jnp.concatenate([yb_ref[0, hh] for hh in range(HEADS)], axis=1).astype(F32)
    y = yag_ref[0].astype(F32) + _sigmoid(gb_ref[0].astype(F32)) * yb
    x = x + mod[2:3] * jnp.dot(y.astype(BF16), wout_ref[...], preferred_element_type=F32)
    h = _rms(x) * (1.0 + mod[4:5]) + mod[3:4]
    gu = jnp.dot(h.astype(BF16), wffi_ref[...], preferred_element_type=F32)
    act = _silu(gu[:, :D_FF]) * gu[:, D_FF:]
    x = x + mod[5:6] * jnp.dot(act.astype(BF16), wffo_ref[...], preferred_element_type=F32)
    if final:
        x = _rms(x) * fg_ref[...]
    o_ref[0] = x


def _mixer_out(x, yag, gb, yb, mod, wout, wffi, wffo, fg, *, tm, final):
    B, S, _ = x.shape
    row = lambda b, s: (b, s, 0)
    return pl.pallas_call(
        functools.partial(_mixer_out_kernel, final=final),
        grid=(B, S // tm),
        in_specs=[pl.BlockSpec((1, tm, D_MODEL), row), pl.BlockSpec((1, tm, D_MODEL), row),
                  pl.BlockSpec((1, tm, D_MODEL), row),
                  pl.BlockSpec((1, HEADS, tm, V_HEAD), lambda b, s: (b, 0, s, 0)),
                  pl.BlockSpec((1, SUBLANES, D_MODEL), lambda b, s: (b, 0, 0)),
                  _const_spec(wout.shape), _const_spec(wffi.shape), _const_spec(wffo.shape),
                  _const_spec(fg.shape)],
        out_specs=pl.BlockSpec((1, tm, D_MODEL), row),
        out_shape=jax.ShapeDtypeStruct((B, S, D_MODEL), F32),
        compiler_params=pltpu.CompilerParams(dimension_semantics=("arbitrary", "arbitrary"),
                                             vmem_limit_bytes=VMEM_LIMIT),
        name="mixer_out",
    )(x, yag, gb, yb, mod, wout, wffi, wffo, fg)


def _rope_cols(w):
    half = QK_ROPE // 2
    x1, x2 = w[..., :half], w[..., half:]
    return jnp.concatenate([x1, x2, x2, x1], axis=-1)


def kernel(x, c, positions, w_ada, b_ada, w_in, conv_w, conv_b, lru_wa, lru_ba, lru_wx, lru_bx,
           lru_a_param, q_norm_g, kv_norm_g, w_uq, w_ukv, w_out, w_ffn_in, w_ffn_out, final_norm_g):
    B, S, _ = x.shape
    depth = w_ada.shape[0]
    tm = min(S, 512)
    tq = min(S, 512)

    c_pad = jnp.zeros((SUBLANES, D_MODEL), F32).at[:B].set(c)
    mod = _adaln_mod(c_pad, w_ada, b_ada)[:, :B].reshape(depth, B, 6, D_MODEL)
    mod = jnp.pad(mod, ((0, 0), (0, 0), (0, SUBLANES - 6), (0, 0)))

    inv_freq = ROPE_THETA ** (-jnp.arange(0, QK_ROPE, 2, dtype=F32) / QK_ROPE)
    rep = LANES // inv_freq.shape[0]
    pos_rep = jnp.repeat(positions.astype(F32).reshape(-1, rep), inv_freq.shape[0], axis=1)
    cos, sin = _rope_tables(pos_rep, jnp.tile(inv_freq, rep)[None, :])
    cos, sin = cos.reshape(B, S, -1), sin.reshape(B, S, -1)
    cs = jnp.concatenate([cos, cos, -sin, sin], axis=-1)
    pos3 = positions.reshape(B, S, 1)

    wq_nope = jnp.transpose(w_uq[..., :QK_NOPE], (0, 2, 1, 3))
    wuk_t = jnp.transpose(w_ukv[..., :QK_NOPE], (0, 2, 3, 1))
    wq_abs = _absorb_q(wq_nope, wuk_t)
    wq_rope = _rope_cols(jnp.transpose(w_uq[..., QK_NOPE:], (0, 2, 1, 3)))
    wq_all = jnp.concatenate([wq_abs, wq_rope], axis=-1)
    wq_all = jnp.transpose(wq_all, (0, 2, 1, 3)).reshape(depth, Q_RANK, HEADS * HEAD_SLOT).astype(BF16)
    wuv = jnp.transpose(w_ukv[..., QK_NOPE:], (0, 2, 1, 3)).astype(BF16)

    o_q, o_kv, o_kr, o_ga, o_gb = (D_MODEL, D_MODEL + Q_RANK, D_MODEL + Q_RANK + KV_RANK,
                                   D_MODEL + Q_RANK + KV_RANK + QK_ROPE,
                                   2 * D_MODEL + Q_RANK + KV_RANK + QK_ROPE)
    win = jnp.concatenate([w_in[..., :o_q], w_in[..., o_ga:o_gb], w_in[..., o_gb:],
                           w_in[..., o_q:o_kv], w_in[..., o_kv:o_kr],
                           _rope_cols(w_in[..., o_kr:o_ga])], axis=-1).astype(BF16)
    wg = jnp.concatenate([lru_wa, lru_wx], axis=-1).astype(BF16)
    convw = jnp.pad(conv_w, ((0, 0), (0, SUBLANES - CONV_W), (0, 0)))
    wout = w_out.astype(BF16)
    wffi = w_ffn_in.astype(BF16)
    wffo = w_ffn_out.astype(BF16)
    fg = final_norm_g.reshape(1, D_MODEL)

    for l in range(depth):
        yag, gb, q, kk, vt = _mixer_in(
            x, mod[l], pos3, cs, win[l], convw[l], conv_b[l].reshape(1, -1), wg[l],
            lru_ba[l].reshape(1, -1), lru_bx[l].reshape(1, -1), lru_a_param[l].reshape(1, -1),
            q_norm_g[l].reshape(1, -1), kv_norm_g[l].reshape(1, -1), wq_all[l], tm=tm)
        yb = _attention(q, kk, vt, wuv[l], tq=tq)
        x = _mixer_out(x, yag, gb, yb, mod[l], wout[l], wffi[l], wffo[l], fg, tm=tm,
                       final=(l == depth - 1))
    return x
```

```python
import functools
import math

import jax
import jax.numpy as jnp
from jax import lax
from jax.experimental import pallas as pl
from jax.experimental.pallas import tpu as pltpu

F32 = jnp.float32
BF16 = jnp.bfloat16

D_MODEL = 1024
LRU_BLOCKS = 8
LRU_BLOCK_W = D_MODEL // LRU_BLOCKS
CONV_W = 4
LRU_C = 8.0
HEADS = 8
QK_NOPE = 128
QK_ROPE = 64
QK_HEAD = QK_NOPE + QK_ROPE
V_HEAD = D_MODEL // HEADS
Q_RANK = 256
KV_RANK = 128
ROPE_THETA = 10000.0
D_FF = -(-8 * D_MODEL // (3 * 256)) * 256
EPS = 1e-6

LANES = 128
SUBLANES = 8
HEAD_SLOT = 2 * LANES
BF16_ROWS = 2 * SUBLANES
V_ROWS = KV_RANK + BF16_ROWS
Q_SCALE = (QK_HEAD ** -0.5) * math.log2(math.e)
VMEM_LIMIT = 56 * 1024 * 1024

C_LRU, C_GA, C_GB, C_QD, C_KV = 0, D_MODEL, 2 * D_MODEL, 3 * D_MODEL, 3 * D_MODEL + Q_RANK
D_IN_PACKED = C_KV + HEAD_SLOT


def _rms(x):
    return x * lax.rsqrt(jnp.mean(x * x, axis=-1, keepdims=True) + EPS)


def _sigmoid(x):
    return 0.5 * jnp.tanh(0.5 * x) + 0.5


def _silu(x):
    return x * _sigmoid(x)


def _const_spec(shape):
    nd = len(shape)
    return pl.BlockSpec(shape, lambda *_: (0,) * nd, pipeline_mode=pl.Buffered(1))


def _mod_kernel(c_ref, w_ref, b_ref, o_ref):
    o_ref[0] = jnp.dot(c_ref[...], w_ref[0], preferred_element_type=F32,
                       precision=lax.Precision.HIGHEST) + b_ref[0]


def _adaln_mod(c_pad, w_ada, b_ada):
    depth, _, n = w_ada.shape
    tn = 1536
    return pl.pallas_call(
        _mod_kernel,
        grid=(depth, n // tn),
        in_specs=[pl.BlockSpec(c_pad.shape, lambda l, j: (0, 0)),
                  pl.BlockSpec((1, D_MODEL, tn), lambda l, j: (l, 0, j)),
                  pl.BlockSpec((1, 1, tn), lambda l, j: (l, 0, j))],
        out_specs=pl.BlockSpec((1, c_pad.shape[0], tn), lambda l, j: (l, 0, j)),
        out_shape=jax.ShapeDtypeStruct((depth, c_pad.shape[0], n), F32),
        compiler_params=pltpu.CompilerParams(dimension_semantics=("arbitrary", "arbitrary")),
        name="adaln_mod",
    )(c_pad, w_ada, b_ada.reshape(depth, 1, n))


def _absorb_kernel(wq_ref, wukt_ref, o_ref):
    o_ref[0, 0] = jnp.dot(wq_ref[0, 0], wukt_ref[0, 0], preferred_element_type=F32,
                          precision=lax.Precision.HIGHEST)


def _absorb_q(wq_nope, wuk_t):
    depth = wq_nope.shape[0]
    return pl.pallas_call(
        _absorb_kernel,
        grid=(depth, HEADS),
        in_specs=[pl.BlockSpec((1, 1, Q_RANK, QK_NOPE), lambda l, h: (l, h, 0, 0)),
                  pl.BlockSpec((1, 1, QK_NOPE, KV_RANK), lambda l, h: (l, h, 0, 0))],
        out_specs=pl.BlockSpec((1, 1, Q_RANK, KV_RANK), lambda l, h: (l, h, 0, 0)),
        out_shape=jax.ShapeDtypeStruct((depth, HEADS, Q_RANK, KV_RANK), F32),
        compiler_params=pltpu.CompilerParams(dimension_semantics=("arbitrary", "arbitrary")),
        name="absorb_q",
    )(wq_nope, wuk_t)


def _rope_kernel(pos_ref, f_ref, cs_ref, cst_ref):
    rows = pos_ref.shape[0]
    half = QK_ROPE // 2
    per = LANES // half
    lane = lax.broadcasted_iota(jnp.int32, (rows, LANES), 1)
    pos = pos_ref[...]
    dense = jnp.zeros((rows, LANES), F32)
    for j in range(per):
        dense = jnp.where(lane // half == j, pos[:, j:j + 1], dense)
    ang = dense * f_ref[...]
    cos, sin = jnp.cos(ang), jnp.sin(ang)
    for j in range(per):
        parts = [pltpu.roll(src, ((g - j) % per) * half, 1) for g, src in enumerate((cos, cos, sin, sin))]
        out = jnp.where(lane < half, parts[0],
                        jnp.where(lane < 2 * half, parts[1],
                                  jnp.where(lane < 3 * half, -parts[2], parts[3])))
        cs_ref[pl.ds(j, rows, stride=per), :] = out
    cst_ref[...] = cs_ref[...].T


def _rope_tables(pos4, freq_rep):
    rows, per = pos4.shape
    tr = min(rows, 512)
    return pl.pallas_call(
        _rope_kernel,
        grid=(rows // tr,),
        in_specs=[pl.BlockSpec((tr, per), lambda i: (i, 0)), pl.BlockSpec((1, LANES), lambda i: (0, 0))],
        out_specs=[pl.BlockSpec((per * tr, LANES), lambda i: (i, 0)),
                   pl.BlockSpec((LANES, per * tr), lambda i: (0, i))],
        out_shape=[jax.ShapeDtypeStruct((per * rows, LANES), F32),
                   jax.ShapeDtypeStruct((LANES, per * rows), F32)],
        compiler_params=pltpu.CompilerParams(dimension_semantics=("arbitrary",)),
        name="rope_tables",
    )(pos4, freq_rep)


def _mixer_in_kernel(x_ref, mod_ref, pos_ref, cs_ref, win_ref, convw_ref, convb_ref, wg_ref,
                     ba_ref, bx_ref, ap_ref, qg_ref, kvg_ref, wqt_ref, cst_ref,
                     yag_ref, gb_ref, q_ref, kk_ref, vt_ref,
                     xe_s, a_s, b_s, hc_s, *, tm):
    @pl.when(pl.program_id(1) == 0)
    def _():
        xe_s[...] = jnp.zeros((SUBLANES, D_MODEL), F32)
        hc_s[...] = jnp.zeros((SUBLANES, D_MODEL), F32)

    x = x_ref[0]
    mod = mod_ref[0]
    h = _rms(x) * (1.0 + mod[1:2]) + mod[0:1]
    proj = jnp.dot(h.astype(BF16), win_ref[...], preferred_element_type=F32)
    xl = proj[:, C_LRU:C_LRU + D_MODEL]
    gate_a = proj[:, C_GA:C_GA + D_MODEL]
    gb_ref[0] = proj[:, C_GB:C_GB + D_MODEL].astype(BF16)
    q_down = proj[:, C_QD:C_QD + Q_RANK]
    kv_slot = proj[:, C_KV:C_KV + HEAD_SLOT]

    cs = cs_ref[0]
    c_q = _rms(q_down) * (qg_ref[...] * Q_SCALE)
    qat = jnp.dot(wqt_ref[...], c_q.T.astype(BF16), preferred_element_type=F32)
    cst = cst_ref[...]
    for hh in range(HEADS):
        base = hh * HEAD_SLOT
        p = qat[base + LANES:base + HEAD_SLOT, :] * cst
        roped = p + pltpu.roll(p, LANES // 2, 0)
        q_ref[0, hh] = jnp.concatenate([qat[base:base + LANES, :], roped], axis=0).astype(BF16)
    c_kv = _rms(kv_slot[:, :KV_RANK]) * kvg_ref[...]
    p = kv_slot[:, KV_RANK:] * cs
    lane = lax.broadcasted_iota(jnp.int32, p.shape, 1)
    k_pe = jnp.where(lane < QK_ROPE, p + pltpu.roll(p, LANES // 2, 1), 0.0)
    kk_ref[0] = jnp.concatenate([c_kv, k_pe], axis=1).astype(BF16)
    vt_ref[0, 0:KV_RANK, :] = c_kv.T.astype(BF16)
    ones_row = lax.broadcasted_iota(jnp.int32, (BF16_ROWS, tm), 0) == 0
    vt_ref[0, KV_RANK:V_ROWS, :] = jnp.where(ones_row, 1.0, 0.0).astype(BF16)

    groups = tm // SUBLANES
    x3 = xl.reshape(groups, SUBLANES, D_MODEL)
    row_d = lax.broadcasted_iota(jnp.int32, (groups, SUBLANES, D_MODEL), 1)
    cw = convw_ref[...]
    xc3 = convb_ref[...] + cw[CONV_W - 1:CONV_W] * x3
    for k in range(1, CONV_W):
        rot = pltpu.roll(x3, k, 1)
        rot_prev = jnp.concatenate([pltpu.roll(xe_s[...][None], k, 1), rot[:-1]], axis=0)
        xc3 = xc3 + cw[CONV_W - 1 - k:CONV_W - k] * jnp.where(row_d < k, rot_prev, rot)
    xe_s[...] = x3[groups - 1]
    xc = xc3.reshape(tm, D_MODEL)

    reset = pos_ref[0] == 0
    xcb = xc.astype(BF16)
    row8 = lax.broadcasted_iota(jnp.int32, (groups, SUBLANES, LRU_BLOCK_W), 1)
    c_softplus = LRU_C * jax.nn.softplus(-ap_ref[...])
    for n in range(LRU_BLOCKS):
        sl = slice(n * LRU_BLOCK_W, (n + 1) * LRU_BLOCK_W)
        g = jnp.dot(xcb[:, sl], wg_ref[n], preferred_element_type=F32)
        r = _sigmoid(g[:, :LRU_BLOCK_W] + ba_ref[:, sl])
        i = _sigmoid(g[:, LRU_BLOCK_W:] + bx_ref[:, sl])
        neg_log_a = c_softplus[:, sl] * r
        a = jnp.exp2((-math.log2(math.e) * c_softplus[:, sl]) * r)
        mult = jnp.sqrt(jnp.tanh(neg_log_a) * (1.0 + a * a))
        xi = xc[:, sl] * i
        b = jnp.where(reset, xi, xi * mult).reshape(groups, SUBLANES, LRU_BLOCK_W)
        a = jnp.where(reset, 0.0, a).reshape(groups, SUBLANES, LRU_BLOCK_W)
        for d in (1, 2, 4):
            keep = row8 >= d
            b = jnp.where(keep, a * pltpu.roll(b, d, 1) + b, b)
            a = jnp.where(keep, a * pltpu.roll(a, d, 1), a)
        a_s[:, sl] = a.reshape(tm, LRU_BLOCK_W)
        b_s[:, sl] = b.reshape(tm, LRU_BLOCK_W)

    def group(gidx, h_prev):
        r0 = pl.multiple_of(gidx * SUBLANES, SUBLANES)
        h8 = b_s[pl.ds(r0, SUBLANES), :] + a_s[pl.ds(r0, SUBLANES), :] * h_prev
        b_s[pl.ds(r0, SUBLANES), :] = h8
        return jnp.broadcast_to(h8[SUBLANES - 1:SUBLANES, :], (SUBLANES, D_MODEL))

    hc_s[...] = lax.fori_loop(0, tm // SUBLANES, group, hc_s[...], unroll=8)
    yag_ref[0] = (_sigmoid(gate_a) * b_s[...]).astype(BF16)


def _mixer_in(x, mod, pos3, cs, win, convw, convb, wg, ba, bx, ap, qg, kvg, wqt, cst, *, tm):
    B, S, _ = x.shape
    row = lambda b, s: (b, s, 0)
    in_specs = [
        pl.BlockSpec((1, tm, D_MODEL), row),
        pl.BlockSpec((1, SUBLANES, D_MODEL), lambda b, s: (b, 0, 0)),
        pl.BlockSpec((1, tm, 1), row),
        pl.BlockSpec((1, tm, LANES), row),
        _const_spec(win.shape), _const_spec(convw.shape), _const_spec(convb.shape),
        _const_spec(wg.shape), _const_spec(ba.shape), _const_spec(bx.shape), _const_spec(ap.shape),
        _const_spec(qg.shape), _const_spec(kvg.shape), _const_spec(wqt.shape),
        pl.BlockSpec((LANES, tm), lambda b, s: (0, b * (S // tm) + s)),
    ]
    out_specs = [
        pl.BlockSpec((1, tm, D_MODEL), row),
        pl.BlockSpec((1, tm, D_MODEL), row),
        pl.BlockSpec((1, HEADS, HEAD_SLOT, tm), lambda b, s: (b, 0, 0, s)),
        pl.BlockSpec((1, tm, HEAD_SLOT), row),
        pl.BlockSpec((1, V_ROWS, tm), lambda b, s: (b, 0, s)),
    ]
    out_shape = [
        jax.ShapeDtypeStruct((B, S, D_MODEL), BF16),
        jax.ShapeDtypeStruct((B, S, D_MODEL), BF16),
        jax.ShapeDtypeStruct((B, HEADS, HEAD_SLOT, S), BF16),
        jax.ShapeDtypeStruct((B, S, HEAD_SLOT), BF16),
        jax.ShapeDtypeStruct((B, V_ROWS, S), BF16),
    ]
    scratch = [pltpu.VMEM((SUBLANES, D_MODEL), F32), pltpu.VMEM((tm, D_MODEL), F32),
               pltpu.VMEM((tm, D_MODEL), F32), pltpu.VMEM((SUBLANES, D_MODEL), F32)]
    return pl.pallas_call(
        functools.partial(_mixer_in_kernel, tm=tm),
        grid=(B, S // tm), in_specs=in_specs, out_specs=out_specs, out_shape=out_shape,
        scratch_shapes=scratch,
        compiler_params=pltpu.CompilerParams(dimension_semantics=("arbitrary", "arbitrary"),
                                             vmem_limit_bytes=VMEM_LIMIT),
        name="mixer_in",
    )(x, mod, pos3, cs, win, convw, convb, wg, ba, bx, ap, qg, kvg, wqt, cst)


def _attn_kernel(q_ref, kk_ref, vt_ref, wuv_ref, o_ref, sa_s, sb_s, mx_s, m_s, acc_s, *, tq):
    qi = pl.program_id(1)
    bufs = (sa_s, sb_s)

    def produce(j, h, buf, mask):
        k0 = pl.multiple_of(j * tq, tq)
        s = jnp.dot(kk_ref[0, pl.ds(k0, tq), :], q_ref[0, h], preferred_element_type=F32)
        if mask is not None:
            kidx = lax.broadcasted_iota(jnp.int32, (tq, tq), 0)
            qidx = lax.broadcasted_iota(jnp.int32, (tq, tq), 1)
            if mask == "any":
                qidx = qidx + (qi - j) * tq
            s = jnp.where(kidx <= qidx, s, -jnp.inf)
        buf[...] = s
        return jnp.max(s, axis=0, keepdims=True)

    def consume(j, h, buf, mx):
        k0 = pl.multiple_of(j * tq, tq)
        m = m_s[h]
        m_new = jnp.maximum(m, mx)
        m_s[h] = m_new
        alpha = jnp.exp2(m - m_new)
        p = jnp.exp2((buf[...] - m_new).astype(BF16))
        vt = vt_ref[0, :, pl.ds(k0, tq)]
        acc_s[h] = alpha * acc_s[h] + jnp.dot(vt, p, preferred_element_type=F32)

    def finalize(h):
        o = acc_s[h, 0:KV_RANK, :] * (1.0 / acc_s[h, KV_RANK:KV_RANK + 1, :])
        o_ref[0, h] = jnp.dot(o.T.astype(BF16), wuv_ref[h], preferred_element_type=F32).astype(BF16)

    def key_block(j, last):
        mx = mx_s[...]
        for h in range(HEADS):
            nxt = bufs[(h + 1) % 2]
            if h + 1 < HEADS:
                mx_next = produce(j, h + 1, nxt, "diag" if last else None)
            elif not last:
                mx_next = produce(j + 1, 0, nxt, "any")
                mx_s[...] = mx_next
            consume(j, h, bufs[h % 2], mx)
            mx = mx_next
            if last:
                finalize(h)

    acc_s[...] = jnp.zeros(acc_s.shape, F32)
    m_s[...] = jnp.full(m_s.shape, -jnp.inf, F32)
    mx_s[...] = produce(0, 0, sa_s, "any")

    def trip(j, c):
        key_block(j, False)
        return c

    lax.fori_loop(0, qi, trip, 0)
    key_block(qi, True)


def _attention(q, kk, vt, wuv, *, tq):
    B, H, _, S = q.shape
    return pl.pallas_call(
        functools.partial(_attn_kernel, tq=tq),
        grid=(B, S // tq),
        in_specs=[pl.BlockSpec((1, H, HEAD_SLOT, tq), lambda b, i: (b, 0, 0, i)),
                  pl.BlockSpec((1, S, HEAD_SLOT), lambda b, i: (b, 0, 0)),
                  pl.BlockSpec((1, V_ROWS, S), lambda b, i: (b, 0, 0)),
                  _const_spec(wuv.shape)],
        out_specs=pl.BlockSpec((1, H, tq, V_HEAD), lambda b, i: (b, 0, i, 0)),
        out_shape=jax.ShapeDtypeStruct((B, H, S, V_HEAD), BF16),
        scratch_shapes=[pltpu.VMEM((tq, tq), F32), pltpu.VMEM((tq, tq), F32),
                        pltpu.VMEM((1, tq), F32), pltpu.VMEM((H, 1, tq), F32),
                        pltpu.VMEM((H, V_ROWS, tq), F32)],
        compiler_params=pltpu.CompilerParams(
            dimension_semantics=("arbitrary", "arbitrary"), vmem_limit_bytes=VMEM_LIMIT),
        name="attention",
    )(q, kk, vt, wuv)


def _mixer_out_kernel(x_ref, yag_ref, gb_ref, yb_ref, mod_ref, wout_ref, wffi_ref, wffo_ref, fg_ref,
                      o_ref, *, final):
    x = x_ref[0]
    mod = mod_ref[0]
    yb = jnp.concatenate([yb_ref[0, hh] for hh in range(HEADS)], axis=1).astype(F32)
    y = yag_ref[0].astype(F32) + _sigmoid(gb_ref[0].astype(F32)) * yb
    x = x + mod[2:3] * jnp.dot(y.astype(BF16), wout_ref[...], preferred_element_type=F32)
    h = _rms(x) * (1.0 + mod[4:5]) + mod[3:4]
    gu = jnp.dot(h.astype(BF16), wffi_ref[...], preferred_element_type=F32)
    act = _silu(gu[:, :D_FF]) * gu[:, D_FF:]
    x = x + mod[5:6] * jnp.dot(act.astype(BF16), wffo_ref[...], preferred_element_type=F32)
    if final:
        x = _rms(x) * fg_ref[...]
    o_ref[0] = x


def _mixer_out(x, yag, gb, yb, mod, wout, wffi, wffo, fg, *, tm, final):
    B, S, _ = x.shape
    row = lambda b, s: (b, s, 0)
    return pl.pallas_call(
        functools.partial(_mixer_out_kernel, final=final),
        grid=(B, S // tm),
        in_specs=[pl.BlockSpec((1, tm, D_MODEL), row), pl.BlockSpec((1, tm, D_MODEL), row),
                  pl.BlockSpec((1, tm, D_MODEL), row),
                  pl.BlockSpec((1, HEADS, tm, V_HEAD), lambda b, s: (b, 0, s, 0)),
                  pl.BlockSpec((1, SUBLANES, D_MODEL), lambda b, s: (b, 0, 0)),
                  _const_spec(wout.shape), _const_spec(wffi.shape), _const_spec(wffo.shape),
                  _const_spec(fg.shape)],
        out_specs=pl.BlockSpec((1, tm, D_MODEL), row),
        out_shape=jax.ShapeDtypeStruct((B, S, D_MODEL), F32),
        compiler_params=pltpu.CompilerParams(dimension_semantics=("arbitrary", "arbitrary"),
                                             vmem_limit_bytes=VMEM_LIMIT),
        name="mixer_out",
    )(x, yag, gb, yb, mod, wout, wffi, wffo, fg)


def _rope_cols(w):
    half = QK_ROPE // 2
    x1, x2 = w[..., :half], w[..., half:]
    return jnp.concatenate([x1, x2, x2, x1], axis=-1)


def kernel(x, c, positions, w_ada, b_ada, w_in, conv_w, conv_b, lru_wa, lru_ba, lru_wx, lru_bx,
           lru_a_param, q_norm_g, kv_norm_g, w_uq, w_ukv, w_out, w_ffn_in, w_ffn_out, final_norm_g):
    B, S, _ = x.shape
    depth = w_ada.shape[0]
    tm = min(S, 512)
    tq = min(S, 512)

    c_pad = jnp.zeros((SUBLANES, D_MODEL), F32).at[:B].set(c)
    mod = _adaln_mod(c_pad, w_ada, b_ada)[:, :B].reshape(depth, B, 6, D_MODEL)
    mod = jnp.pad(mod, ((0, 0), (0, 0), (0, SUBLANES - 6), (0, 0)))

    inv_freq = ROPE_THETA ** (-jnp.arange(0, QK_ROPE, 2, dtype=F32) / QK_ROPE)
    rep = LANES // inv_freq.shape[0]
    cs, cst = _rope_tables(positions.astype(F32).reshape(-1, rep), jnp.tile(inv_freq, rep)[None, :])
    cs = cs.reshape(B, S, LANES)
    pos3 = positions.reshape(B, S, 1)

    wq_nope = jnp.transpose(w_uq[..., :QK_NOPE], (0, 2, 1, 3))
    wuk_t = jnp.transpose(w_ukv[..., :QK_NOPE], (0, 2, 3, 1))
    wq_abs = _absorb_q(wq_nope, wuk_t)
    wq_rope = _rope_cols(jnp.transpose(w_uq[..., QK_NOPE:], (0, 2, 1, 3)))
    wq_all = jnp.concatenate([wq_abs, wq_rope], axis=-1)
    wq_all_t = jnp.transpose(wq_all, (0, 1, 3, 2)).reshape(depth, HEADS * HEAD_SLOT, Q_RANK).astype(BF16)
    wuv = jnp.transpose(w_ukv[..., QK_NOPE:], (0, 2, 1, 3)).astype(BF16)

    o_q, o_kv, o_kr, o_ga, o_gb = (D_MODEL, D_MODEL + Q_RANK, D_MODEL + Q_RANK + KV_RANK,
                                   D_MODEL + Q_RANK + KV_RANK + QK_ROPE,
                                   2 * D_MODEL + Q_RANK + KV_RANK + QK_ROPE)
    win = jnp.concatenate([w_in[..., :o_q], w_in[..., o_ga:o_gb], w_in[..., o_gb:],
                           w_in[..., o_q:o_kv], w_in[..., o_kv:o_kr],
                           _rope_cols(w_in[..., o_kr:o_ga])], axis=-1).astype(BF16)
    wg = jnp.concatenate([lru_wa, lru_wx], axis=-1).astype(BF16)
    convw = jnp.pad(conv_w, ((0, 0), (0, SUBLANES - CONV_W), (0, 0)))
    wout = w_out.astype(BF16)
    wffi = w_ffn_in.astype(BF16)
    wffo = w_ffn_out.astype(BF16)
    fg = final_norm_g.reshape(1, D_MODEL)

    for l in range(depth):
        yag, gb, q, kk, vt = _mixer_in(
            x, mod[l], pos3, cs, win[l], convw[l], conv_b[l].reshape(1, -1), wg[l],
            lru_ba[l].reshape(1, -1), lru_bx[l].reshape(1, -1), lru_a_param[l].reshape(1, -1),
            q_norm_g[l].reshape(1, -1), kv_norm_g[l].reshape(1, -1), wq_all_t[l], cst, tm=tm)
        yb = _attention(q, kk, vt, wuv[l], tq=tq)
        x = _mixer_out(x, yag, gb, yb, mod[l], wout[l], wffi[l], wffo[l], fg, tm=tm,
                       final=(l == depth - 1))
    return x
```

```python
import functools
import math

import jax
import jax.numpy as jnp
from jax import lax
from jax.experimental import pallas as pl
from jax.experimental.pallas import tpu as pltpu

F32 = jnp.float32
BF16 = jnp.bfloat16

D_MODEL = 1024
LRU_BLOCKS = 8
LRU_BLOCK_W = D_MODEL // LRU_BLOCKS
CONV_W = 4
LRU_C = 8.0
HEADS = 8
QK_NOPE = 128
QK_ROPE = 64
QK_HEAD = QK_NOPE + QK_ROPE
V_HEAD = D_MODEL // HEADS
Q_RANK = 256
KV_RANK = 128
ROPE_THETA = 10000.0
D_FF = -(-8 * D_MODEL // (3 * 256)) * 256
EPS = 1e-6

LANES = 128
SUBLANES = 8
HEAD_SLOT = 2 * LANES
BF16_ROWS = 2 * SUBLANES
V_ROWS = KV_RANK + BF16_ROWS
Q_SCALE = (QK_HEAD ** -0.5) * math.log2(math.e)
VMEM_LIMIT = 56 * 1024 * 1024

C_LRU, C_GA, C_GB, C_QD, C_KV = 0, D_MODEL, 2 * D_MODEL, 3 * D_MODEL, 3 * D_MODEL + Q_RANK
D_IN_PACKED = C_KV + HEAD_SLOT
PROJ_CHUNK = 2 * LANES


def _rms(x):
    return x * lax.rsqrt(jnp.mean(x * x, axis=-1, keepdims=True) + EPS)


def _sigmoid(x):
    return 0.5 * jnp.tanh(0.5 * x) + 0.5


def _silu(x):
    return x * _sigmoid(x)


def _const_spec(shape):
    nd = len(shape)
    return pl.BlockSpec(shape, lambda *_: (0,) * nd, pipeline_mode=pl.Buffered(1))


def _mod_kernel(c_ref, w_ref, b_ref, o_ref):
    o_ref[0] = jnp.dot(c_ref[...], w_ref[0], preferred_element_type=F32,
                       precision=lax.Precision.HIGHEST) + b_ref[0]


def _adaln_mod(c_pad, w_ada, b_ada):
    depth, _, n = w_ada.shape
    tn = 1536
    return pl.pallas_call(
        _mod_kernel,
        grid=(depth, n // tn),
        in_specs=[pl.BlockSpec(c_pad.shape, lambda l, j: (0, 0)),
                  pl.BlockSpec((1, D_MODEL, tn), lambda l, j: (l, 0, j)),
                  pl.BlockSpec((1, 1, tn), lambda l, j: (l, 0, j))],
        out_specs=pl.BlockSpec((1, c_pad.shape[0], tn), lambda l, j: (l, 0, j)),
        out_shape=jax.ShapeDtypeStruct((depth, c_pad.shape[0], n), F32),
        compiler_params=pltpu.CompilerParams(dimension_semantics=("arbitrary", "arbitrary")),
        name="adaln_mod",
    )(c_pad, w_ada, b_ada.reshape(depth, 1, n))


def _absorb_kernel(wq_ref, wukt_ref, o_ref):
    o_ref[0, 0] = jnp.dot(wq_ref[0, 0], wukt_ref[0, 0], preferred_element_type=F32,
                          precision=lax.Precision.HIGHEST)


def _absorb_q(wq_nope, wuk_t):
    depth = wq_nope.shape[0]
    return pl.pallas_call(
        _absorb_kernel,
        grid=(depth, HEADS),
        in_specs=[pl.BlockSpec((1, 1, Q_RANK, QK_NOPE), lambda l, h: (l, h, 0, 0)),
                  pl.BlockSpec((1, 1, QK_NOPE, KV_RANK), lambda l, h: (l, h, 0, 0))],
        out_specs=pl.BlockSpec((1, 1, Q_RANK, KV_RANK), lambda l, h: (l, h, 0, 0)),
        out_shape=jax.ShapeDtypeStruct((depth, HEADS, Q_RANK, KV_RANK), F32),
        compiler_params=pltpu.CompilerParams(dimension_semantics=("arbitrary", "arbitrary")),
        name="absorb_q",
    )(wq_nope, wuk_t)


def _rope_kernel(pos_ref, f_ref, cs_ref, cst_ref):
    rows = pos_ref.shape[0]
    half = QK_ROPE // 2
    per = LANES // half
    lane = lax.broadcasted_iota(jnp.int32, (rows, LANES), 1)
    pos = pos_ref[...]
    dense = jnp.zeros((rows, LANES), F32)
    for j in range(per):
        dense = jnp.where(lane // half == j, pos[:, j:j + 1], dense)
    ang = dense * f_ref[...]
    cos, sin = jnp.cos(ang), jnp.sin(ang)
    for j in range(per):
        parts = [pltpu.roll(src, ((g - j) % per) * half, 1) for g, src in enumerate((cos, cos, sin, sin))]
        out = jnp.where(lane < half, parts[0],
                        jnp.where(lane < 2 * half, parts[1],
                                  jnp.where(lane < 3 * half, -parts[2], parts[3])))
        cs_ref[pl.ds(j, rows, stride=per), :] = out
    cst_ref[...] = cs_ref[...].T


def _rope_tables(pos4, freq_rep):
    rows, per = pos4.shape
    tr = min(rows, 512)
    return pl.pallas_call(
        _rope_kernel,
        grid=(rows // tr,),
        in_specs=[pl.BlockSpec((tr, per), lambda i: (i, 0)), pl.BlockSpec((1, LANES), lambda i: (0, 0))],
        out_specs=[pl.BlockSpec((per * tr, LANES), lambda i: (i, 0)),
                   pl.BlockSpec((LANES, per * tr), lambda i: (0, i))],
        out_shape=[jax.ShapeDtypeStruct((per * rows, LANES), F32),
                   jax.ShapeDtypeStruct((LANES, per * rows), F32)],
        compiler_params=pltpu.CompilerParams(dimension_semantics=("arbitrary",)),
        name="rope_tables",
    )(pos4, freq_rep)


def _mixer_in_kernel(x_ref, mod_ref, pos_ref, cs_ref, win_ref, convw_ref, convb_ref, wg_ref,
                     ba_ref, bx_ref, ap_ref, qg_ref, kvg_ref, wqt_ref, cst_ref,
                     yag_ref, gb_ref, q_ref, kk_ref, vt_ref,
                     xe_s, a_s, b_s, sg_s, hc_s, *, tm):
    @pl.when(pl.program_id(1) == 0)
    def _():
        xe_s[...] = jnp.zeros((SUBLANES, D_MODEL), F32)
        hc_s[...] = jnp.zeros((SUBLANES, D_MODEL), F32)

    groups = tm // SUBLANES
    x = x_ref[0]
    mod = mod_ref[0]
    hb = (_rms(x) * (1.0 + mod[1:2]) + mod[0:1]).astype(BF16)
    reset = pos_ref[0] == 0
    c_softplus = LRU_C * jax.nn.softplus(-ap_ref[...])

    def project(c0):
        return jnp.dot(hb, win_ref[:, c0:c0 + PROJ_CHUNK], preferred_element_type=F32)

    def queries(q_down):
        c_q = _rms(q_down) * (qg_ref[...] * Q_SCALE)
        qat = jnp.dot(wqt_ref[...], c_q.T.astype(BF16), preferred_element_type=F32)
        cst = cst_ref[...]
        for hh in range(HEADS):
            base = hh * HEAD_SLOT
            p = qat[base + LANES:base + HEAD_SLOT, :] * cst
            roped = p + pltpu.roll(p, LANES // 2, 0)
            q_ref[0, hh] = jnp.concatenate([qat[base:base + LANES, :], roped], axis=0).astype(BF16)

    def keys(kv_slot):
        cs = cs_ref[0]
        c_kv = _rms(kv_slot[:, :KV_RANK]) * kvg_ref[...]
        p = kv_slot[:, KV_RANK:] * cs
        lane = lax.broadcasted_iota(jnp.int32, p.shape, 1)
        k_pe = jnp.where(lane < QK_ROPE, p + pltpu.roll(p, LANES // 2, 1), 0.0)
        kk_ref[0] = jnp.concatenate([c_kv, k_pe], axis=1).astype(BF16)
        vt_ref[0, 0:KV_RANK, :] = c_kv.T.astype(BF16)
        ones_row = lax.broadcasted_iota(jnp.int32, (BF16_ROWS, tm), 0) == 0
        vt_ref[0, KV_RANK:V_ROWS, :] = jnp.where(ones_row, 1.0, 0.0).astype(BF16)

    def recurrence_inputs(c0, xl):
        cols = slice(c0, c0 + PROJ_CHUNK)
        x3 = xl.reshape(groups, SUBLANES, PROJ_CHUNK)
        row_d = lax.broadcasted_iota(jnp.int32, x3.shape, 1)
        cw = convw_ref[:, cols]
        xc3 = convb_ref[:, cols] + cw[CONV_W - 1:CONV_W] * x3
        for k in range(1, CONV_W):
            rot = pltpu.roll(x3, k, 1)
            rot_prev = jnp.concatenate([pltpu.roll(xe_s[:, cols][None], k, 1), rot[:-1]], axis=0)
            xc3 = xc3 + cw[CONV_W - 1 - k:CONV_W - k] * jnp.where(row_d < k, rot_prev, rot)
        xe_s[:, cols] = x3[groups - 1]
        xc = xc3.reshape(tm, PROJ_CHUNK)
        xcb = xc.astype(BF16)
        row8 = lax.broadcasted_iota(jnp.int32, (groups, SUBLANES, LRU_BLOCK_W), 1)
        for sub in range(PROJ_CHUNK // LRU_BLOCK_W):
            loc = slice(sub * LRU_BLOCK_W, (sub + 1) * LRU_BLOCK_W)
            n = c0 // LRU_BLOCK_W + sub
            sl = slice(n * LRU_BLOCK_W, (n + 1) * LRU_BLOCK_W)
            g = jnp.dot(xcb[:, loc], wg_ref[n], preferred_element_type=F32)
            r = _sigmoid(g[:, :LRU_BLOCK_W] + ba_ref[:, sl])
            i = _sigmoid(g[:, LRU_BLOCK_W:] + bx_ref[:, sl])
            neg_log_a = c_softplus[:, sl] * r
            a = jnp.exp2((-math.log2(math.e) * c_softplus[:, sl]) * r)
            mult = jnp.sqrt(jnp.tanh(neg_log_a) * (1.0 + a * a))
            xi = xc[:, loc] * i
            b = jnp.where(reset, xi, xi * mult).reshape(groups, SUBLANES, LRU_BLOCK_W)
            a = jnp.where(reset, 0.0, a).reshape(groups, SUBLANES, LRU_BLOCK_W)
            for d in (1, 2, 4):
                keep = row8 >= d
                b = jnp.where(keep, a * pltpu.roll(b, d, 1) + b, b)
                a = jnp.where(keep, a * pltpu.roll(a, d, 1), a)
            a_s[:, sl] = a.reshape(tm, LRU_BLOCK_W)
            b_s[:, sl] = b.reshape(tm, LRU_BLOCK_W)

    def recurrence():
        def group(gidx, h_prev):
            r0 = pl.multiple_of(gidx * SUBLANES, SUBLANES)
            h8 = b_s[pl.ds(r0, SUBLANES), :] + a_s[pl.ds(r0, SUBLANES), :] * h_prev
            b_s[pl.ds(r0, SUBLANES), :] = h8
            return jnp.broadcast_to(h8[SUBLANES - 1:SUBLANES, :], (SUBLANES, D_MODEL))

        hc_s[...] = lax.fori_loop(0, groups, group, hc_s[...], unroll=8)

    def project_group(c0):
        return project(C_LRU + c0), project(C_GB + c0), project(C_GA + c0)

    n_group = D_MODEL // PROJ_CHUNK
    nxt = project_group(0)
    for k in range(n_group):
        c0 = k * PROJ_CHUNK
        cur = nxt
        nxt = project_group(c0 + PROJ_CHUNK) if k + 1 < n_group else (project(C_QD), project(C_KV))
        recurrence_inputs(c0, cur[0])
        gb_ref[0, :, c0:c0 + PROJ_CHUNK] = cur[1].astype(BF16)
        sg_s[:, c0:c0 + PROJ_CHUNK] = _sigmoid(cur[2])
    queries(nxt[0])
    keys(nxt[1])
    recurrence()
    yag_ref[0] = (sg_s[...] * b_s[...]).astype(BF16)


def _mixer_in(x, mod, pos3, cs, win, convw, convb, wg, ba, bx, ap, qg, kvg, wqt, cst, *, tm):
    B, S, _ = x.shape
    row = lambda b, s: (b, s, 0)
    in_specs = [
        pl.BlockSpec((1, tm, D_MODEL), row),
        pl.BlockSpec((1, SUBLANES, D_MODEL), lambda b, s: (b, 0, 0)),
        pl.BlockSpec((1, tm, 1), row),
        pl.BlockSpec((1, tm, LANES), row),
        _const_spec(win.shape), _const_spec(convw.shape), _const_spec(convb.shape),
        _const_spec(wg.shape), _const_spec(ba.shape), _const_spec(bx.shape), _const_spec(ap.shape),
        _const_spec(qg.shape), _const_spec(kvg.shape), _const_spec(wqt.shape),
        pl.BlockSpec((LANES, tm), lambda b, s: (0, b * (S // tm) + s)),
    ]
    out_specs = [
        pl.BlockSpec((1, tm, D_MODEL), row),
        pl.BlockSpec((1, tm, D_MODEL), row),
        pl.BlockSpec((1, HEADS, HEAD_SLOT, tm), lambda b, s: (b, 0, 0, s)),
        pl.BlockSpec((1, tm, HEAD_SLOT), row),
        pl.BlockSpec((1, V_ROWS, tm), lambda b, s: (b, 0, s)),
    ]
    out_shape = [
        jax.ShapeDtypeStruct((B, S, D_MODEL), BF16),
        jax.ShapeDtypeStruct((B, S, D_MODEL), BF16),
        jax.ShapeDtypeStruct((B, HEADS, HEAD_SLOT, S), BF16),
        jax.ShapeDtypeStruct((B, S, HEAD_SLOT), BF16),
        jax.ShapeDtypeStruct((B, V_ROWS, S), BF16),
    ]
    scratch = [pltpu.VMEM((SUBLANES, D_MODEL), F32), pltpu.VMEM((tm, D_MODEL), F32),
               pltpu.VMEM((tm, D_MODEL), F32), pltpu.VMEM((tm, D_MODEL), F32),
               pltpu.VMEM((SUBLANES, D_MODEL), F32)]
    return pl.pallas_call(
        functools.partial(_mixer_in_kernel, tm=tm),
        grid=(B, S // tm), in_specs=in_specs, out_specs=out_specs, out_shape=out_shape,
        scratch_shapes=scratch,
        compiler_params=pltpu.CompilerParams(dimension_semantics=("arbitrary", "arbitrary"),
                                             vmem_limit_bytes=VMEM_LIMIT),
        name="mixer_in",
    )(x, mod, pos3, cs, win, convw, convb, wg, ba, bx, ap, qg, kvg, wqt, cst)


def _attn_kernel(q_ref, kk_ref, vt_ref, wuv_ref, o_ref, sa_s, sb_s, mx_s, m_s, acc_s, *, tq):
    qi = pl.program_id(1)
    bufs = (sa_s, sb_s)

    def produce(j, h, buf, mask):
        k0 = pl.multiple_of(j * tq, tq)
        s = jnp.dot(kk_ref[0, pl.ds(k0, tq), :], q_ref[0, h], preferred_element_type=F32)
        if mask is not None:
            kidx = lax.broadcasted_iota(jnp.int32, (tq, tq), 0)
            qidx = lax.broadcasted_iota(jnp.int32, (tq, tq), 1)
            if mask == "any":
                qidx = qidx + (qi - j) * tq
            s = jnp.where(kidx <= qidx, s, -jnp.inf)
        buf[...] = s
        return jnp.max(s, axis=0, keepdims=True)

    def consume(j, h, buf, mx):
        k0 = pl.multiple_of(j * tq, tq)
        m = m_s[h]
        m_new = jnp.maximum(m, mx)
        m_s[h] = m_new
        alpha = jnp.exp2(m - m_new)
        p = jnp.exp2((buf[...] - m_new).astype(BF16))
        vt = vt_ref[0, :, pl.ds(k0, tq)]
        acc_s[h] = alpha * acc_s[h] + jnp.dot(vt, p, preferred_element_type=F32)

    def finalize(h):
        o = acc_s[h, 0:KV_RANK, :] * (1.0 / acc_s[h, KV_RANK:KV_RANK + 1, :])
        o_ref[0, h] = jnp.dot(o.T.astype(BF16), wuv_ref[h], preferred_element_type=F32).astype(BF16)

    def key_block(j, last):
        mx = mx_s[...]
        for h in range(HEADS):
            nxt = bufs[(h + 1) % 2]
            if h + 1 < HEADS:
                mx_next = produce(j, h + 1, nxt, "diag" if last else None)
            elif not last:
                mx_next = produce(j + 1, 0, nxt, "any")
                mx_s[...] = mx_next
            consume(j, h, bufs[h % 2], mx)
            mx = mx_next
            if last:
                finalize(h)

    acc_s[...] = jnp.zeros(acc_s.shape, F32)
    m_s[...] = jnp.full(m_s.shape, -jnp.inf, F32)
    mx_s[...] = produce(0, 0, sa_s, "any")

    def trip(t, c):
        key_block(2 * t, False)
        key_block(2 * t + 1, False)
        return c

    lax.fori_loop(0, qi // 2, trip, 0)

    @pl.when(qi % 2 == 1)
    def _():
        key_block(qi - 1, False)

    key_block(qi, True)


def _attention(q, kk, vt, wuv, *, tq):
    B, H, _, S = q.shape
    return pl.pallas_call(
        functools.partial(_attn_kernel, tq=tq),
        grid=(B, S // tq),
        in_specs=[pl.BlockSpec((1, H, HEAD_SLOT, tq), lambda b, i: (b, 0, 0, i)),
                  pl.BlockSpec((1, S, HEAD_SLOT), lambda b, i: (b, 0, 0)),
                  pl.BlockSpec((1, V_ROWS, S), lambda b, i: (b, 0, 0)),
                  _const_spec(wuv.shape)],
        out_specs=pl.BlockSpec((1, H, tq, V_HEAD), lambda b, i: (b, 0, i, 0)),
        out_shape=jax.ShapeDtypeStruct((B, H, S, V_HEAD), BF16),
        scratch_shapes=[pltpu.VMEM((tq, tq), F32), pltpu.VMEM((tq, tq), F32),
                        pltpu.VMEM((1, tq), F32), pltpu.VMEM((H, 1, tq), F32),
                        pltpu.VMEM((H, V_ROWS, tq), F32)],
        compiler_params=pltpu.CompilerParams(
            dimension_semantics=("arbitrary", "arbitrary"), vmem_limit_bytes=VMEM_LIMIT),
        name="attention",
    )(q, kk, vt, wuv)


def _mixer_out_kernel(x_ref, yag_ref, gb_ref, yb_ref, mod_ref, wout_ref, wffi_ref, wffo_ref, fg_ref,
                      o_ref, *, final):
    x = x_ref[0]
    mod = mod_ref[0]
    yb = jnp.concatenate([yb_ref[0, hh] for hh in range(HEADS)], axis=1).astype(F32)
    y = yag_ref[0].astype(F32) + _sigmoid(gb_ref[0].astype(F32)) * yb
    x = x + mod[2:3] * jnp.dot(y.astype(BF16), wout_ref[...], preferred_element_type=F32)
    h = _rms(x) * (1.0 + mod[4:5]) + mod[3:4]
    gu = jnp.dot(h.astype(BF16), wffi_ref[...], preferred_element_type=F32)
    act = _silu(gu[:, :D_FF]) * gu[:, D_FF:]
    x = x + mod[5:6] * jnp.dot(act.astype(BF16), wffo_ref[...], preferred_element_type=F32)
    if final:
        x = _rms(x) * fg_ref[...]
    o_ref[0] = x


def _mixer_out(x, yag, gb, yb, mod, wout, wffi, wffo, fg, *, tm, final):
    B, S, _ = x.shape
    row = lambda b, s: (b, s, 0)
    return pl.pallas_call(
        functools.partial(_mixer_out_kernel, final=final),
        grid=(B, S // tm),
        in_specs=[pl.BlockSpec((1, tm, D_MODEL), row), pl.BlockSpec((1, tm, D_MODEL), row),
                  pl.BlockSpec((1, tm, D_MODEL), row),
                  pl.BlockSpec((1, HEADS, tm, V_HEAD), lambda b, s: (b, 0, s, 0)),
                  pl.BlockSpec((1, SUBLANES, D_MODEL), lambda b, s: (b, 0, 0)),
                  _const_spec(wout.shape), _const_spec(wffi.shape), _const_spec(wffo.shape),
                  _const_spec(fg.shape)],
        out_specs=pl.BlockSpec((1, tm, D_MODEL), row),
        out_shape=jax.ShapeDtypeStruct((B, S, D_MODEL), F32),
        compiler_params=pltpu.CompilerParams(dimension_semantics=("arbitrary", "arbitrary"),
                                             vmem_limit_bytes=VMEM_LIMIT),
        name="mixer_out",
    )(x, yag, gb, yb, mod, wout, wffi, wffo, fg)


def _rope_cols(w):
    half = QK_ROPE // 2
    x1, x2 = w[..., :half], w[..., half:]
    return jnp.concatenate([x1, x2, x2, x1], axis=-1)


def kernel(x, c, positions, w_ada, b_ada, w_in, conv_w, conv_b, lru_wa, lru_ba, lru_wx, lru_bx,
           lru_a_param, q_norm_g, kv_norm_g, w_uq, w_ukv, w_out, w_ffn_in, w_ffn_out, final_norm_g):
    B, S, _ = x.shape
    depth = w_ada.shape[0]
    tm = min(S, 512)
    tq = min(S, 512)

    c_pad = jnp.zeros((SUBLANES, D_MODEL), F32).at[:B].set(c)
    mod = _adaln_mod(c_pad, w_ada, b_ada)[:, :B].reshape(depth, B, 6, D_MODEL)
    mod = jnp.pad(mod, ((0, 0), (0, 0), (0, SUBLANES - 6), (0, 0)))

    inv_freq = ROPE_THETA ** (-jnp.arange(0, QK_ROPE, 2, dtype=F32) / QK_ROPE)
    rep = LANES // inv_freq.shape[0]
    cs, cst = _rope_tables(positions.astype(F32).reshape(-1, rep), jnp.tile(inv_freq, rep)[None, :])
    cs = cs.reshape(B, S, LANES)
    pos3 = positions.reshape(B, S, 1)

    wq_nope = jnp.transpose(w_uq[..., :QK_NOPE], (0, 2, 1, 3))
    wuk_t = jnp.transpose(w_ukv[..., :QK_NOPE], (0, 2, 3, 1))
    wq_abs = _absorb_q(wq_nope, wuk_t)
    wq_rope = _rope_cols(jnp.transpose(w_uq[..., QK_NOPE:], (0, 2, 1, 3)))
    wq_all = jnp.concatenate([wq_abs, wq_rope], axis=-1)
    wq_all_t = jnp.transpose(wq_all, (0, 1, 3, 2)).reshape(depth, HEADS * HEAD_SLOT, Q_RANK).astype(BF16)
    wuv = jnp.transpose(w_ukv[..., QK_NOPE:], (0, 2, 1, 3)).astype(BF16)

    o_q, o_kv, o_kr, o_ga, o_gb = (D_MODEL, D_MODEL + Q_RANK, D_MODEL + Q_RANK + KV_RANK,
                                   D_MODEL + Q_RANK + KV_RANK + QK_ROPE,
                                   2 * D_MODEL + Q_RANK + KV_RANK + QK_ROPE)
    w_in_b = w_in.astype(BF16)
    win = jnp.concatenate([w_in_b[..., :o_q], w_in_b[..., o_ga:o_gb], w_in_b[..., o_gb:],
                           w_in_b[..., o_q:o_kv], w_in_b[..., o_kv:o_kr],
                           _rope_cols(w_in_b[..., o_kr:o_ga])], axis=-1)
    wg = jnp.concatenate([lru_wa, lru_wx], axis=-1).astype(BF16)
    convw = jnp.pad(conv_w, ((0, 0), (0, SUBLANES - CONV_W), (0, 0)))
    wout = w_out.astype(BF16)
    wffi = w_ffn_in.astype(BF16)
    wffo = w_ffn_out.astype(BF16)
    fg = final_norm_g.reshape(1, D_MODEL)

    for l in range(depth):
        yag, gb, q, kk, vt = _mixer_in(
            x, mod[l], pos3, cs, win[l], convw[l], conv_b[l].reshape(1, -1), wg[l],
            lru_ba[l].reshape(1, -1), lru_bx[l].reshape(1, -1), lru_a_param[l].reshape(1, -1),
            q_norm_g[l].reshape(1, -1), kv_norm_g[l].reshape(1, -1), wq_all_t[l], cst, tm=tm)
        yb = _attention(q, kk, vt, wuv[l], tq=tq)
        x = _mixer_out(x, yag, gb, yb, mod[l], wout[l], wffi[l], wffo[l], fg, tm=tm,
                       final=(l == depth - 1))
    return x
```

```python
import functools
import math

import jax
import jax.numpy as jnp
from jax import lax
from jax.experimental import pallas as pl
from jax.experimental.pallas import tpu as pltpu

F32 = jnp.float32
BF16 = jnp.bfloat16

D_MODEL = 1024
LRU_BLOCKS = 8
LRU_BLOCK_W = D_MODEL // LRU_BLOCKS
CONV_W = 4
LRU_C = 8.0
HEADS = 8
QK_NOPE = 128
QK_ROPE = 64
QK_HEAD = QK_NOPE + QK_ROPE
V_HEAD = D_MODEL // HEADS
Q_RANK = 256
KV_RANK = 128
ROPE_THETA = 10000.0
D_FF = -(-8 * D_MODEL // (3 * 256)) * 256
EPS = 1e-6

LANES = 128
SUBLANES = 8
HEAD_SLOT = 2 * LANES
BF16_ROWS = 2 * SUBLANES
V_ROWS = KV_RANK + BF16_ROWS
Q_SCALE = (QK_HEAD ** -0.5) * math.log2(math.e)
VMEM_LIMIT = 56 * 1024 * 1024

PROJ_CHUNK = 2 * LANES


def _rms(x):
    return x * lax.rsqrt(jnp.mean(x * x, axis=-1, keepdims=True) + EPS)


def _sigmoid(x):
    return 0.5 * jnp.tanh(0.5 * x) + 0.5


def _silu(x):
    return x * _sigmoid(x)


def _const_spec(shape):
    nd = len(shape)
    return pl.BlockSpec(shape, lambda *_: (0,) * nd, pipeline_mode=pl.Buffered(1))


def _mod_kernel(c_ref, w_ref, b_ref, o_ref):
    o_ref[0] = jnp.dot(c_ref[...], w_ref[0], preferred_element_type=F32,
                       precision=lax.Precision.HIGHEST) + b_ref[0]


def _adaln_mod(c_pad, w_ada, b_ada):
    depth, _, n = w_ada.shape
    tn = 1536
    return pl.pallas_call(
        _mod_kernel,
        grid=(depth, n // tn),
        in_specs=[pl.BlockSpec(c_pad.shape, lambda l, j: (0, 0)),
                  pl.BlockSpec((1, D_MODEL, tn), lambda l, j: (l, 0, j)),
                  pl.BlockSpec((1, 1, tn), lambda l, j: (l, 0, j))],
        out_specs=pl.BlockSpec((1, c_pad.shape[0], tn), lambda l, j: (l, 0, j)),
        out_shape=jax.ShapeDtypeStruct((depth, c_pad.shape[0], n), F32),
        compiler_params=pltpu.CompilerParams(dimension_semantics=("arbitrary", "arbitrary")),
        name="adaln_mod",
    )(c_pad, w_ada, b_ada.reshape(depth, 1, n))


def _absorb_kernel(wq_ref, wukt_ref, o_ref):
    o_ref[0, 0] = jnp.dot(wq_ref[0, 0], wukt_ref[0, 0], preferred_element_type=F32,
                          precision=lax.Precision.HIGHEST)


def _absorb_q(wq_nope, wuk_t):
    depth = wq_nope.shape[0]
    return pl.pallas_call(
        _absorb_kernel,
        grid=(depth, HEADS),
        in_specs=[pl.BlockSpec((1, 1, Q_RANK, QK_NOPE), lambda l, h: (l, h, 0, 0)),
                  pl.BlockSpec((1, 1, QK_NOPE, KV_RANK), lambda l, h: (l, h, 0, 0))],
        out_specs=pl.BlockSpec((1, 1, Q_RANK, KV_RANK), lambda l, h: (l, h, 0, 0)),
        out_shape=jax.ShapeDtypeStruct((depth, HEADS, Q_RANK, KV_RANK), F32),
        compiler_params=pltpu.CompilerParams(dimension_semantics=("arbitrary", "arbitrary")),
        name="absorb_q",
    )(wq_nope, wuk_t)


def _rope_kernel(pos_ref, f_ref, cs_ref, cst_ref):
    rows = pos_ref.shape[0]
    half = QK_ROPE // 2
    per = LANES // half
    lane = lax.broadcasted_iota(jnp.int32, (rows, LANES), 1)
    pos = pos_ref[...]
    dense = jnp.zeros((rows, LANES), F32)
    for j in range(per):
        dense = jnp.where(lane // half == j, pos[:, j:j + 1], dense)
    ang = dense * f_ref[...]
    cos, sin = jnp.cos(ang), jnp.sin(ang)
    for j in range(per):
        parts = [pltpu.roll(src, ((g - j) % per) * half, 1) for g, src in enumerate((cos, cos, sin, sin))]
        out = jnp.where(lane < half, parts[0],
                        jnp.where(lane < 2 * half, parts[1],
                                  jnp.where(lane < 3 * half, -parts[2], parts[3])))
        cs_ref[pl.ds(j, rows, stride=per), :] = out
    cst_ref[...] = cs_ref[...].T


def _rope_tables(pos4, freq_rep):
    rows, per = pos4.shape
    tr = min(rows, 512)
    return pl.pallas_call(
        _rope_kernel,
        grid=(rows // tr,),
        in_specs=[pl.BlockSpec((tr, per), lambda i: (i, 0)), pl.BlockSpec((1, LANES), lambda i: (0, 0))],
        out_specs=[pl.BlockSpec((per * tr, LANES), lambda i: (i, 0)),
                   pl.BlockSpec((LANES, per * tr), lambda i: (0, i))],
        out_shape=[jax.ShapeDtypeStruct((per * rows, LANES), F32),
                   jax.ShapeDtypeStruct((LANES, per * rows), F32)],
        compiler_params=pltpu.CompilerParams(dimension_semantics=("arbitrary",)),
        name="rope_tables",
    )(pos4, freq_rep)


def _mixer_in_kernel(x_ref, mod_ref, pos_ref, cs_ref, wl_ref, wga_ref, wgb_ref, wqd_ref, wkv_ref,
                     convw_ref, convb_ref, wg_ref,
                     ba_ref, bx_ref, ap_ref, qg_ref, kvg_ref, wqt_ref, cst_ref,
                     yag_ref, gb_ref, q_ref, kk_ref, vt_ref,
                     xe_s, a_s, b_s, sg_s, hc_s, *, tm):
    @pl.when(pl.program_id(1) == 0)
    def _():
        xe_s[...] = jnp.zeros((SUBLANES, D_MODEL), F32)
        hc_s[...] = jnp.zeros((SUBLANES, D_MODEL), F32)

    groups = tm // SUBLANES
    x = x_ref[0]
    mod = mod_ref[0]
    hb = (_rms(x) * (1.0 + mod[1:2]) + mod[0:1]).astype(BF16)
    reset = pos_ref[0] == 0
    c_softplus = LRU_C * jax.nn.softplus(-ap_ref[...])

    def project(w_ref, c0):
        return jnp.dot(hb, w_ref[:, c0:c0 + PROJ_CHUNK], preferred_element_type=F32)

    def queries(q_down):
        c_q = _rms(q_down) * (qg_ref[...] * Q_SCALE)
        qat = jnp.dot(wqt_ref[...], c_q.T.astype(BF16), preferred_element_type=F32)
        cst = cst_ref[...]
        for hh in range(HEADS):
            base = hh * HEAD_SLOT
            p = qat[base + LANES:base + HEAD_SLOT, :] * cst
            roped = p + pltpu.roll(p, LANES // 2, 0)
            q_ref[0, hh] = jnp.concatenate([qat[base:base + LANES, :], roped], axis=0).astype(BF16)

    def keys(kv_slot):
        cs = cs_ref[0]
        c_kv = _rms(kv_slot[:, :KV_RANK]) * kvg_ref[...]
        p = kv_slot[:, KV_RANK:] * cs
        lane = lax.broadcasted_iota(jnp.int32, p.shape, 1)
        k_pe = jnp.where(lane < QK_ROPE, p + pltpu.roll(p, LANES // 2, 1), 0.0)
        kk_ref[0] = jnp.concatenate([c_kv, k_pe], axis=1).astype(BF16)
        vt_ref[0, 0:KV_RANK, :] = c_kv.T.astype(BF16)
        ones_row = lax.broadcasted_iota(jnp.int32, (BF16_ROWS, tm), 0) == 0
        vt_ref[0, KV_RANK:V_ROWS, :] = jnp.where(ones_row, 1.0, 0.0).astype(BF16)

    def recurrence_inputs(c0, xl, issue_next):
        cols = slice(c0, c0 + PROJ_CHUNK)
        x3 = xl.reshape(groups, SUBLANES, PROJ_CHUNK)
        row_d = lax.broadcasted_iota(jnp.int32, x3.shape, 1)
        cw = convw_ref[:, cols]
        xc3 = convb_ref[:, cols] + cw[CONV_W - 1:CONV_W] * x3
        for k in range(1, CONV_W):
            rot = pltpu.roll(x3, k, 1)
            rot_prev = jnp.concatenate([pltpu.roll(xe_s[:, cols][None], k, 1), rot[:-1]], axis=0)
            xc3 = xc3 + cw[CONV_W - 1 - k:CONV_W - k] * jnp.where(row_d < k, rot_prev, rot)
        xe_s[:, cols] = x3[groups - 1]
        xc = xc3.reshape(tm, PROJ_CHUNK)
        xcb = xc.astype(BF16)
        row8 = lax.broadcasted_iota(jnp.int32, (groups, SUBLANES, LRU_BLOCK_W), 1)
        n_sub = PROJ_CHUNK // LRU_BLOCK_W
        gates = [jnp.dot(xcb[:, sub * LRU_BLOCK_W:(sub + 1) * LRU_BLOCK_W], wg_ref[c0 // LRU_BLOCK_W + sub],
                         preferred_element_type=F32) for sub in range(n_sub)]
        issued = issue_next()
        for sub in range(n_sub):
            loc = slice(sub * LRU_BLOCK_W, (sub + 1) * LRU_BLOCK_W)
            n = c0 // LRU_BLOCK_W + sub
            sl = slice(n * LRU_BLOCK_W, (n + 1) * LRU_BLOCK_W)
            g = gates[sub]
            r = _sigmoid(g[:, :LRU_BLOCK_W] + ba_ref[:, sl])
            i = _sigmoid(g[:, LRU_BLOCK_W:] + bx_ref[:, sl])
            neg_log_a = c_softplus[:, sl] * r
            a = jnp.exp2((-math.log2(math.e) * c_softplus[:, sl]) * r)
            mult = jnp.sqrt(jnp.tanh(neg_log_a) * (1.0 + a * a))
            xi = xc[:, loc] * i
            b = jnp.where(reset, xi, xi * mult).reshape(groups, SUBLANES, LRU_BLOCK_W)
            a = jnp.where(reset, 0.0, a).reshape(groups, SUBLANES, LRU_BLOCK_W)
            for d in (1, 2, 4):
                keep = row8 >= d
                b = jnp.where(keep, a * pltpu.roll(b, d, 1) + b, b)
                a = jnp.where(keep, a * pltpu.roll(a, d, 1), a)
            a_s[:, sl] = a.reshape(tm, LRU_BLOCK_W)
            b_s[:, sl] = b.reshape(tm, LRU_BLOCK_W)
        return issued

    def recurrence():
        def group(gidx, h_prev):
            r0 = pl.multiple_of(gidx * SUBLANES, SUBLANES)
            h8 = b_s[pl.ds(r0, SUBLANES), :] + a_s[pl.ds(r0, SUBLANES), :] * h_prev
            b_s[pl.ds(r0, SUBLANES), :] = h8
            return jnp.broadcast_to(h8[SUBLANES - 1:SUBLANES, :], (SUBLANES, D_MODEL))

        hc_s[...] = lax.fori_loop(0, groups, group, hc_s[...], unroll=8)

    def project_group(c0):
        return project(wl_ref, c0), project(wgb_ref, c0), project(wga_ref, c0)

    n_group = D_MODEL // PROJ_CHUNK
    attn_after = n_group // 2 - 1
    cur = project_group(0)
    for k in range(n_group):
        c0 = k * PROJ_CHUNK
        if k == attn_after:
            issue_next = lambda: (project(wqd_ref, 0), project(wkv_ref, 0))
        elif k + 1 < n_group:
            issue_next = functools.partial(project_group, c0 + PROJ_CHUNK)
        else:
            issue_next = lambda: None
        nxt = recurrence_inputs(c0, cur[0], issue_next)
        gb_ref[0, :, c0:c0 + PROJ_CHUNK] = cur[1].astype(BF16)
        sg_s[:, c0:c0 + PROJ_CHUNK] = _sigmoid(cur[2])
        if k == attn_after:
            after = project_group(c0 + PROJ_CHUNK)
            queries(nxt[0])
            keys(nxt[1])
            nxt = after
        cur = nxt
    recurrence()
    yag_ref[0] = (sg_s[...] * b_s[...]).astype(BF16)


def _mixer_in(x, mod, pos3, cs, wins, convw, convb, wg, ba, bx, ap, qg, kvg, wqt, cst, *, tm):
    B, S, _ = x.shape
    row = lambda b, s: (b, s, 0)
    in_specs = [
        pl.BlockSpec((1, tm, D_MODEL), row),
        pl.BlockSpec((1, SUBLANES, D_MODEL), lambda b, s: (b, 0, 0)),
        pl.BlockSpec((1, tm, 1), row),
        pl.BlockSpec((1, tm, LANES), row),
        *[_const_spec(w.shape) for w in wins], _const_spec(convw.shape), _const_spec(convb.shape),
        _const_spec(wg.shape), _const_spec(ba.shape), _const_spec(bx.shape), _const_spec(ap.shape),
        _const_spec(qg.shape), _const_spec(kvg.shape), _const_spec(wqt.shape),
        pl.BlockSpec((LANES, tm), lambda b, s: (0, b * (S // tm) + s)),
    ]
    out_specs = [
        pl.BlockSpec((1, tm, D_MODEL), row),
        pl.BlockSpec((1, tm, D_MODEL), row),
        pl.BlockSpec((1, HEADS, HEAD_SLOT, tm), lambda b, s: (b, 0, 0, s)),
        pl.BlockSpec((1, tm, HEAD_SLOT), row),
        pl.BlockSpec((1, V_ROWS, tm), lambda b, s: (b, 0, s)),
    ]
    out_shape = [
        jax.ShapeDtypeStruct((B, S, D_MODEL), BF16),
        jax.ShapeDtypeStruct((B, S, D_MODEL), BF16),
        jax.ShapeDtypeStruct((B, HEADS, HEAD_SLOT, S), BF16),
        jax.ShapeDtypeStruct((B, S, HEAD_SLOT), BF16),
        jax.ShapeDtypeStruct((B, V_ROWS, S), BF16),
    ]
    scratch = [pltpu.VMEM((SUBLANES, D_MODEL), F32), pltpu.VMEM((tm, D_MODEL), F32),
               pltpu.VMEM((tm, D_MODEL), F32), pltpu.VMEM((tm, D_MODEL), F32),
               pltpu.VMEM((SUBLANES, D_MODEL), F32)]
    return pl.pallas_call(
        functools.partial(_mixer_in_kernel, tm=tm),
        grid=(B, S // tm), in_specs=in_specs, out_specs=out_specs, out_shape=out_shape,
        scratch_shapes=scratch,
        compiler_params=pltpu.CompilerParams(dimension_semantics=("arbitrary", "arbitrary"),
                                             vmem_limit_bytes=VMEM_LIMIT),
        name="mixer_in",
    )(x, mod, pos3, cs, *wins, convw, convb, wg, ba, bx, ap, qg, kvg, wqt, cst)


def _attn_kernel(q_ref, kk_ref, vt_ref, wuv_ref, o_ref, sa_s, sb_s, mx_s, m_s, acc_s, *, tq):
    qi = pl.program_id(1)
    bufs = (sa_s, sb_s)

    def produce(j, h, buf, mask):
        k0 = pl.multiple_of(j * tq, tq)
        s = jnp.dot(kk_ref[0, pl.ds(k0, tq), :], q_ref[0, h], preferred_element_type=F32)
        if mask is not None:
            kidx = lax.broadcasted_iota(jnp.int32, (tq, tq), 0)
            qidx = lax.broadcasted_iota(jnp.int32, (tq, tq), 1)
            if mask == "any":
                qidx = qidx + (qi - j) * tq
            s = jnp.where(kidx <= qidx, s, -jnp.inf)
        buf[...] = s
        return jnp.max(s, axis=0, keepdims=True)

    def consume(j, h, buf, mx):
        k0 = pl.multiple_of(j * tq, tq)
        m = m_s[h]
        m_new = jnp.maximum(m, mx)
        m_s[h] = m_new
        alpha = jnp.exp2(m - m_new)
        p = jnp.exp2((buf[...] - m_new).astype(BF16))
        vt = vt_ref[0, :, pl.ds(k0, tq)]
        acc_s[h] = alpha * acc_s[h] + jnp.dot(vt, p, preferred_element_type=F32)

    def finalize(h):
        o = acc_s[h, 0:KV_RANK, :] * (1.0 / acc_s[h, KV_RANK:KV_RANK + 1, :])
        o_ref[0, h] = jnp.dot(o.T.astype(BF16), wuv_ref[h], preferred_element_type=F32).astype(BF16)

    def key_block(j, last):
        mx = mx_s[...]
        for h in range(HEADS):
            nxt = bufs[(h + 1) % 2]
            if h + 1 < HEADS:
                mx_next = produce(j, h + 1, nxt, "diag" if last else None)
            elif not last:
                mx_next = produce(j + 1, 0, nxt, "any")
                mx_s[...] = mx_next
            consume(j, h, bufs[h % 2], mx)
            mx = mx_next
            if last:
                finalize(h)

    acc_s[...] = jnp.zeros(acc_s.shape, F32)
    m_s[...] = jnp.full(m_s.shape, -jnp.inf, F32)
    mx_s[...] = produce(0, 0, sa_s, "any")

    def trip(t, c):
        key_block(2 * t, False)
        key_block(2 * t + 1, False)
        return c

    lax.fori_loop(0, qi // 2, trip, 0)

    @pl.when(qi % 2 == 1)
    def _():
        key_block(qi - 1, False)

    key_block(qi, True)


def _attention(q, kk, vt, wuv, *, tq):
    B, H, _, S = q.shape
    return pl.pallas_call(
        functools.partial(_attn_kernel, tq=tq),
        grid=(B, S // tq),
        in_specs=[pl.BlockSpec((1, H, HEAD_SLOT, tq), lambda b, i: (b, 0, 0, i)),
                  pl.BlockSpec((1, S, HEAD_SLOT), lambda b, i: (b, 0, 0)),
                  pl.BlockSpec((1, V_ROWS, S), lambda b, i: (b, 0, 0)),
                  _const_spec(wuv.shape)],
        out_specs=pl.BlockSpec((1, H, tq, V_HEAD), lambda b, i: (b, 0, i, 0)),
        out_shape=jax.ShapeDtypeStruct((B, H, S, V_HEAD), BF16),
        scratch_shapes=[pltpu.VMEM((tq, tq), F32), pltpu.VMEM((tq, tq), F32),
                        pltpu.VMEM((1, tq), F32), pltpu.VMEM((H, 1, tq), F32),
                        pltpu.VMEM((H, V_ROWS, tq), F32)],
        compiler_params=pltpu.CompilerParams(
            dimension_semantics=("arbitrary", "arbitrary"), vmem_limit_bytes=VMEM_LIMIT),
        name="attention",
    )(q, kk, vt, wuv)


def _mixer_out_kernel(x_ref, yag_ref, gb_ref, yb_ref, mod_ref, wout_ref, wffi_ref, wffo_ref, fg_ref,
                      o_ref, *, final):
    x = x_ref[0]
    mod = mod_ref[0]
    yb = jnp.concatenate([yb_ref[0, hh] for hh in range(HEADS)], axis=1).astype(F32)
    y = yag_ref[0].astype(F32) + _sigmoid(gb_ref[0].astype(F32)) * yb
    x = x + mod[2:3] * jnp.dot(y.astype(BF16), wout_ref[...], preferred_element_type=F32)
    h = _rms(x) * (1.0 + mod[4:5]) + mod[3:4]
    gu = jnp.dot(h.astype(BF16), wffi_ref[...], preferred_element_type=F32)
    act = _silu(gu[:, :D_FF]) * gu[:, D_FF:]
    x = x + mod[5:6] * jnp.dot(act.astype(BF16), wffo_ref[...], preferred_element_type=F32)
    if final:
        x = _rms(x) * fg_ref[...]
    o_ref[0] = x


def _mixer_out(x, yag, gb, yb, mod, wout, wffi, wffo, fg, *, tm, final):
    B, S, _ = x.shape
    row = lambda b, s: (b, s, 0)
    return pl.pallas_call(
        functools.partial(_mixer_out_kernel, final=final),
        grid=(B, S // tm),
        in_specs=[pl.BlockSpec((1, tm, D_MODEL), row), pl.BlockSpec((1, tm, D_MODEL), row),
                  pl.BlockSpec((1, tm, D_MODEL), row),
                  pl.BlockSpec((1, HEADS, tm, V_HEAD), lambda b, s: (b, 0, s, 0)),
                  pl.BlockSpec((1, SUBLANES, D_MODEL), lambda b, s: (b, 0, 0)),
                  _const_spec(wout.shape), _const_spec(wffi.shape), _const_spec(wffo.shape),
                  _const_spec(fg.shape)],
        out_specs=pl.BlockSpec((1, tm, D_MODEL), row),
        out_shape=jax.ShapeDtypeStruct((B, S, D_MODEL), F32),
        compiler_params=pltpu.CompilerParams(dimension_semantics=("arbitrary", "arbitrary"),
                                             vmem_limit_bytes=VMEM_LIMIT),
        name="mixer_out",
    )(x, yag, gb, yb, mod, wout, wffi, wffo, fg)


def _rope_cols(w):
    half = QK_ROPE // 2
    x1, x2 = w[..., :half], w[..., half:]
    return jnp.concatenate([x1, x2, x2, x1], axis=-1)


def kernel(x, c, positions, w_ada, b_ada, w_in, conv_w, conv_b, lru_wa, lru_ba, lru_wx, lru_bx,
           lru_a_param, q_norm_g, kv_norm_g, w_uq, w_ukv, w_out, w_ffn_in, w_ffn_out, final_norm_g):
    B, S, _ = x.shape
    depth = w_ada.shape[0]
    tm = min(S, 512)
    tq = min(S, 512)

    c_pad = jnp.zeros((SUBLANES, D_MODEL), F32).at[:B].set(c)
    mod = _adaln_mod(c_pad, w_ada, b_ada)[:, :B].reshape(depth, B, 6, D_MODEL)
    mod = jnp.pad(mod, ((0, 0), (0, 0), (0, SUBLANES - 6), (0, 0)))

    inv_freq = ROPE_THETA ** (-jnp.arange(0, QK_ROPE, 2, dtype=F32) / QK_ROPE)
    rep = LANES // inv_freq.shape[0]
    cs, cst = _rope_tables(positions.astype(F32).reshape(-1, rep), jnp.tile(inv_freq, rep)[None, :])
    cs = cs.reshape(B, S, LANES)
    pos3 = positions.reshape(B, S, 1)

    wq_nope = jnp.transpose(w_uq[..., :QK_NOPE], (0, 2, 1, 3))
    wuk_t = jnp.transpose(w_ukv[..., :QK_NOPE], (0, 2, 3, 1))
    wq_abs = _absorb_q(wq_nope, wuk_t)
    wq_rope = _rope_cols(jnp.transpose(w_uq[..., QK_NOPE:], (0, 2, 1, 3)))
    wq_all = jnp.concatenate([wq_abs, wq_rope], axis=-1)
    wq_all_t = jnp.transpose(wq_all, (0, 1, 3, 2)).reshape(depth, HEADS * HEAD_SLOT, Q_RANK).astype(BF16)
    wuv = jnp.transpose(w_ukv[..., QK_NOPE:], (0, 2, 1, 3)).astype(BF16)

    o_q, o_kv, o_kr, o_ga, o_gb = (D_MODEL, D_MODEL + Q_RANK, D_MODEL + Q_RANK + KV_RANK,
                                   D_MODEL + Q_RANK + KV_RANK + QK_ROPE,
                                   2 * D_MODEL + Q_RANK + KV_RANK + QK_ROPE)
    wins = (w_in[..., :o_q].astype(BF16), w_in[..., o_ga:o_gb].astype(BF16), w_in[..., o_gb:].astype(BF16),
            w_in[..., o_q:o_kv].astype(BF16),
            jnp.concatenate([w_in[..., o_kv:o_kr], _rope_cols(w_in[..., o_kr:o_ga])], axis=-1).astype(BF16))
    wg = jnp.concatenate([lru_wa, lru_wx], axis=-1).astype(BF16)
    convw = jnp.pad(conv_w, ((0, 0), (0, SUBLANES - CONV_W), (0, 0)))
    wout = w_out.astype(BF16)
    wffi = w_ffn_in.astype(BF16)
    wffo = w_ffn_out.astype(BF16)
    fg = final_norm_g.reshape(1, D_MODEL)

    for l in range(depth):
        yag, gb, q, kk, vt = _mixer_in(
            x, mod[l], pos3, cs, [w[l] for w in wins], convw[l], conv_b[l].reshape(1, -1), wg[l],
            lru_ba[l].reshape(1, -1), lru_bx[l].reshape(1, -1), lru_a_param[l].reshape(1, -1),
            q_norm_g[l].reshape(1, -1), kv_norm_g[l].reshape(1, -1), wq_all_t[l], cst, tm=tm)
        yb = _attention(q, kk, vt, wuv[l], tq=tq)
        x = _mixer_out(x, yag, gb, yb, mod[l], wout[l], wffi[l], wffo[l], fg, tm=tm,
                       final=(l == depth - 1))
    return x
```

```python
import functools
import math

import jax
import jax.numpy as jnp
from jax import lax
from jax.experimental import pallas as pl
from jax.experimental.pallas import tpu as pltpu

F32 = jnp.float32
BF16 = jnp.bfloat16

D_MODEL = 1024
LRU_BLOCKS = 8
LRU_BLOCK_W = D_MODEL // LRU_BLOCKS
CONV_W = 4
LRU_C = 8.0
HEADS = 8
QK_NOPE = 128
QK_ROPE = 64
QK_HEAD = QK_NOPE + QK_ROPE
V_HEAD = D_MODEL // HEADS
Q_RANK = 256
KV_RANK = 128
ROPE_THETA = 10000.0
D_FF = -(-8 * D_MODEL // (3 * 256)) * 256
EPS = 1e-6

LANES = 128
SUBLANES = 8
HEAD_SLOT = 2 * LANES
BF16_ROWS = 2 * SUBLANES
V_ROWS = KV_RANK + BF16_ROWS
Q_SCALE = (QK_HEAD ** -0.5) * math.log2(math.e)
VMEM_LIMIT = 56 * 1024 * 1024

PROJ_CHUNK = 2 * LANES


def _rms(x):
    return x * lax.rsqrt(jnp.mean(x * x, axis=-1, keepdims=True) + EPS)


def _sigmoid(x):
    return 0.5 * jnp.tanh(0.5 * x) + 0.5


def _silu(x):
    return x * _sigmoid(x)


def _const_spec(shape):
    nd = len(shape)
    return pl.BlockSpec(shape, lambda *_: (0,) * nd, pipeline_mode=pl.Buffered(1))


def _mod_kernel(c_ref, w_ref, b_ref, o_ref):
    o_ref[0] = jnp.dot(c_ref[...], w_ref[0], preferred_element_type=F32,
                       precision=lax.Precision.HIGHEST) + b_ref[0]


def _adaln_mod(c_pad, w_ada, b_ada):
    depth, _, n = w_ada.shape
    tn = 1536
    return pl.pallas_call(
        _mod_kernel,
        grid=(depth, n // tn),
        in_specs=[pl.BlockSpec(c_pad.shape, lambda l, j: (0, 0)),
                  pl.BlockSpec((1, D_MODEL, tn), lambda l, j: (l, 0, j)),
                  pl.BlockSpec((1, 1, tn), lambda l, j: (l, 0, j))],
        out_specs=pl.BlockSpec((1, c_pad.shape[0], tn), lambda l, j: (l, 0, j)),
        out_shape=jax.ShapeDtypeStruct((depth, c_pad.shape[0], n), F32),
        compiler_params=pltpu.CompilerParams(dimension_semantics=("arbitrary", "arbitrary")),
        name="adaln_mod",
    )(c_pad, w_ada, b_ada.reshape(depth, 1, n))


def _absorb_kernel(wq_ref, wukt_ref, o_ref):
    o_ref[0, 0] = jnp.dot(wq_ref[0, 0], wukt_ref[0, 0], preferred_element_type=F32,
                          precision=lax.Precision.HIGHEST)


def _absorb_q(wq_nope, wuk_t):
    depth = wq_nope.shape[0]
    return pl.pallas_call(
        _absorb_kernel,
        grid=(depth, HEADS),
        in_specs=[pl.BlockSpec((1, 1, Q_RANK, QK_NOPE), lambda l, h: (l, h, 0, 0)),
                  pl.BlockSpec((1, 1, QK_NOPE, KV_RANK), lambda l, h: (l, h, 0, 0))],
        out_specs=pl.BlockSpec((1, 1, Q_RANK, KV_RANK), lambda l, h: (l, h, 0, 0)),
        out_shape=jax.ShapeDtypeStruct((depth, HEADS, Q_RANK, KV_RANK), F32),
        compiler_params=pltpu.CompilerParams(dimension_semantics=("arbitrary", "arbitrary")),
        name="absorb_q",
    )(wq_nope, wuk_t)


def _rope_kernel(pos_ref, f_ref, cs_ref, cst_ref):
    rows = pos_ref.shape[0]
    half = QK_ROPE // 2
    per = LANES // half
    lane = lax.broadcasted_iota(jnp.int32, (rows, LANES), 1)
    pos = pos_ref[...]
    dense = jnp.zeros((rows, LANES), F32)
    for j in range(per):
        dense = jnp.where(lane // half == j, pos[:, j:j + 1], dense)
    ang = dense * f_ref[...]
    cos, sin = jnp.cos(ang), jnp.sin(ang)
    for j in range(per):
        parts = [pltpu.roll(src, ((g - j) % per) * half, 1) for g, src in enumerate((cos, cos, sin, sin))]
        out = jnp.where(lane < half, parts[0],
                        jnp.where(lane < 2 * half, parts[1],
                                  jnp.where(lane < 3 * half, -parts[2], parts[3])))
        cs_ref[pl.ds(j, rows, stride=per), :] = out
    cst_ref[...] = cs_ref[...].T


def _rope_tables(pos4, freq_rep):
    rows, per = pos4.shape
    tr = min(rows, 512)
    return pl.pallas_call(
        _rope_kernel,
        grid=(rows // tr,),
        in_specs=[pl.BlockSpec((tr, per), lambda i: (i, 0)), pl.BlockSpec((1, LANES), lambda i: (0, 0))],
        out_specs=[pl.BlockSpec((per * tr, LANES), lambda i: (i, 0)),
                   pl.BlockSpec((LANES, per * tr), lambda i: (0, i))],
        out_shape=[jax.ShapeDtypeStruct((per * rows, LANES), F32),
                   jax.ShapeDtypeStruct((LANES, per * rows), F32)],
        compiler_params=pltpu.CompilerParams(dimension_semantics=("arbitrary",)),
        name="rope_tables",
    )(pos4, freq_rep)


def _mixer_in_kernel(x_ref, mod_ref, pos_ref, cs_ref, wl_ref, wga_ref, wgb_ref, wqd_ref, wkv_ref,
                     convw_ref, convb_ref, wg_ref,
                     ba_ref, bx_ref, ap_ref, qg_ref, kvg_ref, wqt_ref, cst_ref,
                     yag_ref, gb_ref, q_ref, kk_ref, vt_ref,
                     xe_s, a_s, b_s, sg_s, hc_s, *, tm):
    @pl.when(pl.program_id(1) == 0)
    def _():
        xe_s[...] = jnp.zeros((SUBLANES, D_MODEL), F32)
        hc_s[...] = jnp.zeros((SUBLANES, D_MODEL), F32)

    groups = tm // SUBLANES
    x = x_ref[0]
    mod = mod_ref[0]
    hb = (_rms(x) * (1.0 + mod[1:2]) + mod[0:1]).astype(BF16)
    reset = pos_ref[0] == 0
    c_softplus = LRU_C * jax.nn.softplus(-ap_ref[...])

    def project(w_ref, c0):
        return jnp.dot(hb, w_ref[:, c0:c0 + PROJ_CHUNK], preferred_element_type=F32)

    def queries(q_down):
        c_q = _rms(q_down) * (qg_ref[...] * Q_SCALE)
        qat = jnp.dot(wqt_ref[...], c_q.T.astype(BF16), preferred_element_type=F32)
        cst = cst_ref[...]
        for hh in range(HEADS):
            base = hh * HEAD_SLOT
            p = qat[base + LANES:base + HEAD_SLOT, :] * cst
            roped = p + pltpu.roll(p, LANES // 2, 0)
            q_ref[0, hh] = jnp.concatenate([qat[base:base + LANES, :], roped], axis=0).astype(BF16)

    def keys(kv_slot):
        cs = cs_ref[0]
        c_kv = _rms(kv_slot[:, :KV_RANK]) * kvg_ref[...]
        p = kv_slot[:, KV_RANK:] * cs
        lane = lax.broadcasted_iota(jnp.int32, p.shape, 1)
        k_pe = jnp.where(lane < QK_ROPE, p + pltpu.roll(p, LANES // 2, 1), 0.0)
        kk_ref[0] = jnp.concatenate([c_kv, k_pe], axis=1).astype(BF16)
        vt_ref[0, 0:KV_RANK, :] = c_kv.T.astype(BF16)
        ones_row = lax.broadcasted_iota(jnp.int32, (BF16_ROWS, tm), 0) == 0
        vt_ref[0, KV_RANK:V_ROWS, :] = jnp.where(ones_row, 1.0, 0.0).astype(BF16)

    def recurrence_inputs(c0, xl, issue_next):
        cols = slice(c0, c0 + PROJ_CHUNK)
        x3 = xl.reshape(groups, SUBLANES, PROJ_CHUNK)
        row_d = lax.broadcasted_iota(jnp.int32, x3.shape, 1)
        cw = convw_ref[:, cols]
        xc3 = convb_ref[:, cols] + cw[CONV_W - 1:CONV_W] * x3
        for k in range(1, CONV_W):
            rot = pltpu.roll(x3, k, 1)
            rot_prev = jnp.concatenate([pltpu.roll(xe_s[:, cols][None], k, 1), rot[:-1]], axis=0)
            xc3 = xc3 + cw[CONV_W - 1 - k:CONV_W - k] * jnp.where(row_d < k, rot_prev, rot)
        xe_s[:, cols] = x3[groups - 1]
        xc = xc3.reshape(tm, PROJ_CHUNK)
        xcb = xc.astype(BF16)
        row8 = lax.broadcasted_iota(jnp.int32, (groups, SUBLANES, LRU_BLOCK_W), 1)
        n_sub = PROJ_CHUNK // LRU_BLOCK_W
        gates = [jnp.dot(xcb[:, sub * LRU_BLOCK_W:(sub + 1) * LRU_BLOCK_W], wg_ref[c0 // LRU_BLOCK_W + sub],
                         preferred_element_type=F32) for sub in range(n_sub)]
        issued = issue_next()
        for sub in range(n_sub):
            loc = slice(sub * LRU_BLOCK_W, (sub + 1) * LRU_BLOCK_W)
            n = c0 // LRU_BLOCK_W + sub
            sl = slice(n * LRU_BLOCK_W, (n + 1) * LRU_BLOCK_W)
            g = gates[sub]
            r = _sigmoid(g[:, :LRU_BLOCK_W] + ba_ref[:, sl])
            i = _sigmoid(g[:, LRU_BLOCK_W:] + bx_ref[:, sl])
            neg_log_a = c_softplus[:, sl] * r
            a = jnp.exp2((-math.log2(math.e) * c_softplus[:, sl]) * r)
            mult = jnp.sqrt(jnp.tanh(neg_log_a) * (1.0 + a * a))
            xi = xc[:, loc] * i
            b = jnp.where(reset, xi, xi * mult).reshape(groups, SUBLANES, LRU_BLOCK_W)
            a = jnp.where(reset, 0.0, a).reshape(groups, SUBLANES, LRU_BLOCK_W)
            for d in (1, 2, 4):
                keep = row8 >= d
                b = jnp.where(keep, a * pltpu.roll(b, d, 1) + b, b)
                a = jnp.where(keep, a * pltpu.roll(a, d, 1), a)
            a_s[:, sl] = a.reshape(tm, LRU_BLOCK_W)
            b_s[:, sl] = b.reshape(tm, LRU_BLOCK_W)
        return issued

    def recurrence():
        def group(gidx, h_prev):
            r0 = pl.multiple_of(gidx * SUBLANES, SUBLANES)
            h8 = b_s[pl.ds(r0, SUBLANES), :] + a_s[pl.ds(r0, SUBLANES), :] * h_prev
            b_s[pl.ds(r0, SUBLANES), :] = h8
            return jnp.broadcast_to(h8[SUBLANES - 1:SUBLANES, :], (SUBLANES, D_MODEL))

        hc_s[...] = lax.fori_loop(0, groups, group, hc_s[...], unroll=8)

    def project_group(c0):
        return project(wl_ref, c0), project(wgb_ref, c0), project(wga_ref, c0)

    n_group = D_MODEL // PROJ_CHUNK
    attn_after = n_group // 2 - 1
    cur = project_group(0)
    for k in range(n_group):
        c0 = k * PROJ_CHUNK
        if k == attn_after:
            issue_next = lambda: (project(wqd_ref, 0), project(wkv_ref, 0))
        elif k + 1 < n_group:
            issue_next = functools.partial(project_group, c0 + PROJ_CHUNK)
        else:
            issue_next = lambda: None
        nxt = recurrence_inputs(c0, cur[0], issue_next)
        gb_ref[0, :, c0:c0 + PROJ_CHUNK] = cur[1].astype(BF16)
        sg_s[:, c0:c0 + PROJ_CHUNK] = _sigmoid(cur[2])
        if k == attn_after:
            after = project_group(c0 + PROJ_CHUNK)
            queries(nxt[0])
            keys(nxt[1])
            nxt = after
        cur = nxt
    recurrence()
    yag_ref[0] = (sg_s[...] * b_s[...]).astype(BF16)


def _mixer_in(x, mod, pos3, cs, wins, convw, convb, wg, ba, bx, ap, qg, kvg, wqt, cst, *, tm):
    B, S, _ = x.shape
    row = lambda b, s: (b, s, 0)
    in_specs = [
        pl.BlockSpec((1, tm, D_MODEL), row),
        pl.BlockSpec((1, SUBLANES, D_MODEL), lambda b, s: (b, 0, 0)),
        pl.BlockSpec((1, tm, 1), row),
        pl.BlockSpec((1, tm, LANES), row),
        *[_const_spec(w.shape) for w in wins], _const_spec(convw.shape), _const_spec(convb.shape),
        _const_spec(wg.shape), _const_spec(ba.shape), _const_spec(bx.shape), _const_spec(ap.shape),
        _const_spec(qg.shape), _const_spec(kvg.shape), _const_spec(wqt.shape),
        pl.BlockSpec((LANES, tm), lambda b, s: (0, b * (S // tm) + s)),
    ]
    out_specs = [
        pl.BlockSpec((1, tm, D_MODEL), row),
        pl.BlockSpec((1, tm, D_MODEL), row),
        pl.BlockSpec((1, HEADS, HEAD_SLOT, tm), lambda b, s: (b, 0, 0, s)),
        pl.BlockSpec((1, tm, HEAD_SLOT), row),
        pl.BlockSpec((1, V_ROWS, tm), lambda b, s: (b, 0, s)),
    ]
    out_shape = [
        jax.ShapeDtypeStruct((B, S, D_MODEL), BF16),
        jax.ShapeDtypeStruct((B, S, D_MODEL), BF16),
        jax.ShapeDtypeStruct((B, HEADS, HEAD_SLOT, S), BF16),
        jax.ShapeDtypeStruct((B, S, HEAD_SLOT), BF16),
        jax.ShapeDtypeStruct((B, V_ROWS, S), BF16),
    ]
    scratch = [pltpu.VMEM((SUBLANES, D_MODEL), F32), pltpu.VMEM((tm, D_MODEL), F32),
               pltpu.VMEM((tm, D_MODEL), F32), pltpu.VMEM((tm, D_MODEL), F32),
               pltpu.VMEM((SUBLANES, D_MODEL), F32)]
    return pl.pallas_call(
        functools.partial(_mixer_in_kernel, tm=tm),
        grid=(B, S // tm), in_specs=in_specs, out_specs=out_specs, out_shape=out_shape,
        scratch_shapes=scratch,
        compiler_params=pltpu.CompilerParams(dimension_semantics=("arbitrary", "arbitrary"),
                                             vmem_limit_bytes=VMEM_LIMIT),
        name="mixer_in",
    )(x, mod, pos3, cs, *wins, convw, convb, wg, ba, bx, ap, qg, kvg, wqt, cst)


def _attn_kernel(q_ref, kk_ref, vt_ref, wuv_ref, o_ref, sa_s, sb_s, mx_s, m_s, acc_s, *, tq):
    qi = pl.program_id(1)
    bufs = (sa_s, sb_s)

    def produce(j, h, buf, mask):
        k0 = pl.multiple_of(j * tq, tq)
        s = jnp.dot(kk_ref[0, pl.ds(k0, tq), :], q_ref[0, h], preferred_element_type=F32)
        if mask is not None:
            kidx = lax.broadcasted_iota(jnp.int32, (tq, tq), 0)
            qidx = lax.broadcasted_iota(jnp.int32, (tq, tq), 1)
            if mask == "any":
                qidx = qidx + (qi - j) * tq
            s = jnp.where(kidx <= qidx, s, -jnp.inf)
        buf[...] = s
        return jnp.max(s, axis=0, keepdims=True)

    def consume(j, h, buf, mx):
        k0 = pl.multiple_of(j * tq, tq)
        m = m_s[h]
        m_new = jnp.maximum(m, mx)
        m_s[h] = m_new
        alpha = jnp.exp2(m - m_new)
        p = jnp.exp2((buf[...] - m_new).astype(BF16))
        vt = vt_ref[0, :, pl.ds(k0, tq)]
        acc_s[h] = alpha * acc_s[h] + jnp.dot(vt, p, preferred_element_type=F32)

    def finalize(h):
        o = acc_s[h, 0:KV_RANK, :] * (1.0 / acc_s[h, KV_RANK:KV_RANK + 1, :])
        o_ref[0, h] = jnp.dot(o.T.astype(BF16), wuv_ref[h], preferred_element_type=F32).astype(BF16)

    def key_block(j, last):
        mx = mx_s[...]
        for h in range(HEADS):
            nxt = bufs[(h + 1) % 2]
            if h + 1 < HEADS:
                mx_next = produce(j, h + 1, nxt, "diag" if last else None)
            elif not last:
                mx_next = produce(j + 1, 0, nxt, "any")
                mx_s[...] = mx_next
            consume(j, h, bufs[h % 2], mx)
            mx = mx_next
            if last and h > 0:
                finalize(h - 1)
        if last:
            finalize(HEADS - 1)

    acc_s[...] = jnp.zeros(acc_s.shape, F32)
    m_s[...] = jnp.full(m_s.shape, -jnp.inf, F32)
    mx_s[...] = produce(0, 0, sa_s, "any")

    def trip(t, c):
        key_block(2 * t, False)
        key_block(2 * t + 1, False)
        return c

    lax.fori_loop(0, qi // 2, trip, 0)

    @pl.when(qi % 2 == 1)
    def _():
        key_block(qi - 1, False)

    key_block(qi, True)


def _attention(q, kk, vt, wuv, *, tq):
    B, H, _, S = q.shape
    return pl.pallas_call(
        functools.partial(_attn_kernel, tq=tq),
        grid=(B, S // tq),
        in_specs=[pl.BlockSpec((1, H, HEAD_SLOT, tq), lambda b, i: (b, 0, 0, i)),
                  pl.BlockSpec((1, S, HEAD_SLOT), lambda b, i: (b, 0, 0)),
                  pl.BlockSpec((1, V_ROWS, S), lambda b, i: (b, 0, 0)),
                  _const_spec(wuv.shape)],
        out_specs=pl.BlockSpec((1, H, tq, V_HEAD), lambda b, i: (b, 0, i, 0)),
        out_shape=jax.ShapeDtypeStruct((B, H, S, V_HEAD), BF16),
        scratch_shapes=[pltpu.VMEM((tq, tq), F32), pltpu.VMEM((tq, tq), F32),
                        pltpu.VMEM((1, tq), F32), pltpu.VMEM((H, 1, tq), F32),
                        pltpu.VMEM((H, V_ROWS, tq), F32)],
        compiler_params=pltpu.CompilerParams(
            dimension_semantics=("arbitrary", "arbitrary"), vmem_limit_bytes=VMEM_LIMIT),
        name="attention",
    )(q, kk, vt, wuv)


def _mixer_out_kernel(x_ref, yag_ref, gb_ref, yb_ref, mod_ref, wout_ref, wffi_ref, wffo_ref, fg_ref,
                      o_ref, *, final):
    x = x_ref[0]
    mod = mod_ref[0]
    yb = jnp.concatenate([yb_ref[0, hh] for hh in range(HEADS)], axis=1).astype(F32)
    y = yag_ref[0].astype(F32) + _sigmoid(gb_ref[0].astype(F32)) * yb
    x = x + mod[2:3] * jnp.dot(y.astype(BF16), wout_ref[...], preferred_element_type=F32)
    h = _rms(x) * (1.0 + mod[4:5]) + mod[3:4]
    gu = jnp.dot(h.astype(BF16), wffi_ref[...], preferred_element_type=F32)
    act = _silu(gu[:, :D_FF]) * gu[:, D_FF:]
    x = x + mod[5:6] * jnp.dot(act.astype(BF16), wffo_ref[...], preferred_element_type=F32)
    if final:
        x = _rms(x) * fg_ref[...]
    o_ref[0] = x


def _mixer_out(x, yag, gb, yb, mod, wout, wffi, wffo, fg, *, tm, final):
    B, S, _ = x.shape
    row = lambda b, s: (b, s, 0)
    return pl.pallas_call(
        functools.partial(_mixer_out_kernel, final=final),
        grid=(B, S // tm),
        in_specs=[pl.BlockSpec((1, tm, D_MODEL), row), pl.BlockSpec((1, tm, D_MODEL), row),
                  pl.BlockSpec((1, tm, D_MODEL), row),
                  pl.BlockSpec((1, HEADS, tm, V_HEAD), lambda b, s: (b, 0, s, 0)),
                  pl.BlockSpec((1, SUBLANES, D_MODEL), lambda b, s: (b, 0, 0)),
                  _const_spec(wout.shape), _const_spec(wffi.shape), _const_spec(wffo.shape),
                  _const_spec(fg.shape)],
        out_specs=pl.BlockSpec((1, tm, D_MODEL), row),
        out_shape=jax.ShapeDtypeStruct((B, S, D_MODEL), F32),
        compiler_params=pltpu.CompilerParams(dimension_semantics=("arbitrary", "arbitrary"),
                                             vmem_limit_bytes=VMEM_LIMIT),
        name="mixer_out",
    )(x, yag, gb, yb, mod, wout, wffi, wffo, fg)


def _rope_cols(w):
    half = QK_ROPE // 2
    x1, x2 = w[..., :half], w[..., half:]
    return jnp.concatenate([x1, x2, x2, x1], axis=-1)


def kernel(x, c, positions, w_ada, b_ada, w_in, conv_w, conv_b, lru_wa, lru_ba, lru_wx, lru_bx,
           lru_a_param, q_norm_g, kv_norm_g, w_uq, w_ukv, w_out, w_ffn_in, w_ffn_out, final_norm_g):
    B, S, _ = x.shape
    depth = w_ada.shape[0]
    tm = min(S, 512)
    tq = min(S, 512)

    c_pad = jnp.zeros((SUBLANES, D_MODEL), F32).at[:B].set(c)
    mod = _adaln_mod(c_pad, w_ada, b_ada)[:, :B].reshape(depth, B, 6, D_MODEL)
    mod = jnp.pad(mod, ((0, 0), (0, 0), (0, SUBLANES - 6), (0, 0)))

    inv_freq = ROPE_THETA ** (-jnp.arange(0, QK_ROPE, 2, dtype=F32) / QK_ROPE)
    rep = LANES // inv_freq.shape[0]
    cs, cst = _rope_tables(positions.astype(F32).reshape(-1, rep), jnp.tile(inv_freq, rep)[None, :])
    cs = cs.reshape(B, S, LANES)
    pos3 = positions.reshape(B, S, 1)

    wq_nope = jnp.transpose(w_uq[..., :QK_NOPE], (0, 2, 1, 3))
    wuk_t = jnp.transpose(w_ukv[..., :QK_NOPE], (0, 2, 3, 1))
    wq_abs = _absorb_q(wq_nope, wuk_t)
    wq_rope = _rope_cols(jnp.transpose(w_uq[..., QK_NOPE:], (0, 2, 1, 3)))
    wq_all = jnp.concatenate([wq_abs, wq_rope], axis=-1)
    wq_all_t = jnp.transpose(wq_all, (0, 1, 3, 2)).reshape(depth, HEADS * HEAD_SLOT, Q_RANK).astype(BF16)
    wuv = jnp.transpose(w_ukv[..., QK_NOPE:], (0, 2, 1, 3)).astype(BF16)

    o_q, o_kv, o_kr, o_ga, o_gb = (D_MODEL, D_MODEL + Q_RANK, D_MODEL + Q_RANK + KV_RANK,
                                   D_MODEL + Q_RANK + KV_RANK + QK_ROPE,
                                   2 * D_MODEL + Q_RANK + KV_RANK + QK_ROPE)
    wins = (w_in[..., :o_q].astype(BF16), w_in[..., o_ga:o_gb].astype(BF16), w_in[..., o_gb:].astype(BF16),
            w_in[..., o_q:o_kv].astype(BF16),
            jnp.concatenate([w_in[..., o_kv:o_kr], _rope_cols(w_in[..., o_kr:o_ga])], axis=-1).astype(BF16))
    wg = jnp.concatenate([lru_wa, lru_wx], axis=-1).astype(BF16)
    convw = jnp.pad(conv_w, ((0, 0), (0, SUBLANES - CONV_W), (0, 0)))
    wout = w_out.astype(BF16)
    wffi = w_ffn_in.astype(BF16)
    wffo = w_ffn_out.astype(BF16)
    fg = final_norm_g.reshape(1, D_MODEL)

    for l in range(depth):
        yag, gb, q, kk, vt = _mixer_in(
            x, mod[l], pos3, cs, [w[l] for w in wins], convw[l], conv_b[l].reshape(1, -1), wg[l],
            lru_ba[l].reshape(1, -1), lru_bx[l].reshape(1, -1), lru_a_param[l].reshape(1, -1),
            q_norm_g[l].reshape(1, -1), kv_norm_g[l].reshape(1, -1), wq_all_t[l], cst, tm=tm)
        yb = _attention(q, kk, vt, wuv[l], tq=tq)
        x = _mixer_out(x, yag, gb, yb, mod[l], wout[l], wffi[l], wffo[l], fg, tm=tm,
                       final=(l == depth - 1))
    return x
```

```python
import functools
import math

import jax
import jax.numpy as jnp
from jax import lax
from jax.experimental import pallas as pl
from jax.experimental.pallas import tpu as pltpu

F32 = jnp.float32
BF16 = jnp.bfloat16

D_MODEL = 1024
LRU_BLOCKS = 8
LRU_BLOCK_W = D_MODEL // LRU_BLOCKS
CONV_W = 4
LRU_C = 8.0
HEADS = 8
QK_NOPE = 128
QK_ROPE = 64
QK_HEAD = QK_NOPE + QK_ROPE
V_HEAD = D_MODEL // HEADS
Q_RANK = 256
KV_RANK = 128
ROPE_THETA = 10000.0
D_FF = -(-8 * D_MODEL // (3 * 256)) * 256
EPS = 1e-6

LANES = 128
SUBLANES = 8
HEAD_SLOT = 2 * LANES
BF16_ROWS = 2 * SUBLANES
V_ROWS = KV_RANK + BF16_ROWS
Q_SCALE = (QK_HEAD ** -0.5) * math.log2(math.e)
VMEM_LIMIT = 56 * 1024 * 1024

PROJ_CHUNK = 2 * LANES
LOOKAHEAD = 2


def _rms(x):
    return x * lax.rsqrt(jnp.mean(x * x, axis=-1, keepdims=True) + EPS)


def _sigmoid(x):
    return 0.5 * jnp.tanh(0.5 * x) + 0.5


def _silu(x):
    return x * _sigmoid(x)


def _const_spec(shape):
    nd = len(shape)
    return pl.BlockSpec(shape, lambda *_: (0,) * nd, pipeline_mode=pl.Buffered(1))


def _mod_kernel(c_ref, w_ref, b_ref, o_ref):
    o_ref[0] = jnp.dot(c_ref[...], w_ref[0], preferred_element_type=F32,
                       precision=lax.Precision.HIGHEST) + b_ref[0]


def _adaln_mod(c_pad, w_ada, b_ada):
    depth, _, n = w_ada.shape
    tn = 1536
    return pl.pallas_call(
        _mod_kernel,
        grid=(depth, n // tn),
        in_specs=[pl.BlockSpec(c_pad.shape, lambda l, j: (0, 0)),
                  pl.BlockSpec((1, D_MODEL, tn), lambda l, j: (l, 0, j)),
                  pl.BlockSpec((1, 1, tn), lambda l, j: (l, 0, j))],
        out_specs=pl.BlockSpec((1, c_pad.shape[0], tn), lambda l, j: (l, 0, j)),
        out_shape=jax.ShapeDtypeStruct((depth, c_pad.shape[0], n), F32),
        compiler_params=pltpu.CompilerParams(dimension_semantics=("arbitrary", "arbitrary")),
        name="adaln_mod",
    )(c_pad, w_ada, b_ada.reshape(depth, 1, n))


def _absorb_kernel(wq_ref, wukt_ref, o_ref):
    o_ref[0, 0] = jnp.dot(wq_ref[0, 0], wukt_ref[0, 0], preferred_element_type=F32,
                          precision=lax.Precision.HIGHEST)


def _absorb_q(wq_nope, wuk_t):
    depth = wq_nope.shape[0]
    return pl.pallas_call(
        _absorb_kernel,
        grid=(depth, HEADS),
        in_specs=[pl.BlockSpec((1, 1, Q_RANK, QK_NOPE), lambda l, h: (l, h, 0, 0)),
                  pl.BlockSpec((1, 1, QK_NOPE, KV_RANK), lambda l, h: (l, h, 0, 0))],
        out_specs=pl.BlockSpec((1, 1, Q_RANK, KV_RANK), lambda l, h: (l, h, 0, 0)),
        out_shape=jax.ShapeDtypeStruct((depth, HEADS, Q_RANK, KV_RANK), F32),
        compiler_params=pltpu.CompilerParams(dimension_semantics=("arbitrary", "arbitrary")),
        name="absorb_q",
    )(wq_nope, wuk_t)


def _rope_kernel(pos_ref, f_ref, cs_ref, cst_ref):
    rows = pos_ref.shape[0]
    half = QK_ROPE // 2
    per = LANES // half
    lane = lax.broadcasted_iota(jnp.int32, (rows, LANES), 1)
    pos = pos_ref[...]
    dense = jnp.zeros((rows, LANES), F32)
    for j in range(per):
        dense = jnp.where(lane // half == j, pos[:, j:j + 1], dense)
    ang = dense * f_ref[...]
    cos, sin = jnp.cos(ang), jnp.sin(ang)
    for j in range(per):
        parts = [pltpu.roll(src, ((g - j) % per) * half, 1) for g, src in enumerate((cos, cos, sin, sin))]
        out = jnp.where(lane < half, parts[0],
                        jnp.where(lane < 2 * half, parts[1],
                                  jnp.where(lane < 3 * half, -parts[2], parts[3])))
        cs_ref[pl.ds(j, rows, stride=per), :] = out
    cst_ref[...] = cs_ref[...].T


def _rope_tables(pos4, freq_rep):
    rows, per = pos4.shape
    tr = min(rows, 512)
    return pl.pallas_call(
        _rope_kernel,
        grid=(rows // tr,),
        in_specs=[pl.BlockSpec((tr, per), lambda i: (i, 0)), pl.BlockSpec((1, LANES), lambda i: (0, 0))],
        out_specs=[pl.BlockSpec((per * tr, LANES), lambda i: (i, 0)),
                   pl.BlockSpec((LANES, per * tr), lambda i: (0, i))],
        out_shape=[jax.ShapeDtypeStruct((per * rows, LANES), F32),
                   jax.ShapeDtypeStruct((LANES, per * rows), F32)],
        compiler_params=pltpu.CompilerParams(dimension_semantics=("arbitrary",)),
        name="rope_tables",
    )(pos4, freq_rep)


def _mixer_in_kernel(x_ref, mod_ref, pos_ref, cs_ref, wl_ref, wga_ref, wgb_ref, wqd_ref, wkv_ref,
                     convw_ref, convb_ref, wg_ref,
                     ba_ref, bx_ref, ap_ref, qg_ref, kvg_ref, wqt_ref, cst_ref,
                     yag_ref, gb_ref, q_ref, kk_ref, vt_ref,
                     xe_s, a_s, b_s, sg_s, hc_s, *, tm):
    @pl.when(pl.program_id(1) == 0)
    def _():
        xe_s[...] = jnp.zeros((SUBLANES, D_MODEL), F32)
        hc_s[...] = jnp.zeros((SUBLANES, D_MODEL), F32)

    groups = tm // SUBLANES
    x = x_ref[0]
    mod = mod_ref[0]
    hb = (_rms(x) * (1.0 + mod[1:2]) + mod[0:1]).astype(BF16)
    reset = pos_ref[0] == 0
    c_softplus = LRU_C * jax.nn.softplus(-ap_ref[...])

    def project(w_ref, c0):
        return jnp.dot(hb, w_ref[:, c0:c0 + PROJ_CHUNK], preferred_element_type=F32)

    def queries(q_down):
        c_q = _rms(q_down) * (qg_ref[...] * Q_SCALE)
        qat = jnp.dot(wqt_ref[...], c_q.T.astype(BF16), preferred_element_type=F32)
        cst = cst_ref[...]
        for hh in range(HEADS):
            base = hh * HEAD_SLOT
            p = qat[base + LANES:base + HEAD_SLOT, :] * cst
            roped = p + pltpu.roll(p, LANES // 2, 0)
            q_ref[0, hh] = jnp.concatenate([qat[base:base + LANES, :], roped], axis=0).astype(BF16)

    def keys(kv_slot):
        cs = cs_ref[0]
        c_kv = _rms(kv_slot[:, :KV_RANK]) * kvg_ref[...]
        p = kv_slot[:, KV_RANK:] * cs
        lane = lax.broadcasted_iota(jnp.int32, p.shape, 1)
        k_pe = jnp.where(lane < QK_ROPE, p + pltpu.roll(p, LANES // 2, 1), 0.0)
        kk_ref[0] = jnp.concatenate([c_kv, k_pe], axis=1).astype(BF16)
        vt_ref[0, 0:KV_RANK, :] = c_kv.T.astype(BF16)
        ones_row = lax.broadcasted_iota(jnp.int32, (BF16_ROWS, tm), 0) == 0
        vt_ref[0, KV_RANK:V_ROWS, :] = jnp.where(ones_row, 1.0, 0.0).astype(BF16)

    def recurrence_inputs(c0, xl, issue_next):
        cols = slice(c0, c0 + PROJ_CHUNK)
        x3 = xl.reshape(groups, SUBLANES, PROJ_CHUNK)
        row_d = lax.broadcasted_iota(jnp.int32, x3.shape, 1)
        cw = convw_ref[:, cols]
        xc3 = convb_ref[:, cols] + cw[CONV_W - 1:CONV_W] * x3
        for k in range(1, CONV_W):
            rot = pltpu.roll(x3, k, 1)
            rot_prev = jnp.concatenate([pltpu.roll(xe_s[:, cols][None], k, 1), rot[:-1]], axis=0)
            xc3 = xc3 + cw[CONV_W - 1 - k:CONV_W - k] * jnp.where(row_d < k, rot_prev, rot)
        xe_s[:, cols] = x3[groups - 1]
        xc = xc3.reshape(tm, PROJ_CHUNK)
        xcb = xc.astype(BF16)
        row8 = lax.broadcasted_iota(jnp.int32, (groups, SUBLANES, LRU_BLOCK_W), 1)
        n_sub = PROJ_CHUNK // LRU_BLOCK_W
        gates = [jnp.dot(xcb[:, sub * LRU_BLOCK_W:(sub + 1) * LRU_BLOCK_W], wg_ref[c0 // LRU_BLOCK_W + sub],
                         preferred_element_type=F32) for sub in range(n_sub)]
        issued = issue_next()
        for sub in range(n_sub):
            loc = slice(sub * LRU_BLOCK_W, (sub + 1) * LRU_BLOCK_W)
            n = c0 // LRU_BLOCK_W + sub
            sl = slice(n * LRU_BLOCK_W, (n + 1) * LRU_BLOCK_W)
            g = gates[sub]
            r = _sigmoid(g[:, :LRU_BLOCK_W] + ba_ref[:, sl])
            i = _sigmoid(g[:, LRU_BLOCK_W:] + bx_ref[:, sl])
            neg_log_a = c_softplus[:, sl] * r
            a = jnp.exp2((-math.log2(math.e) * c_softplus[:, sl]) * r)
            mult = jnp.sqrt(jnp.tanh(neg_log_a) * (1.0 + a * a))
            xi = xc[:, loc] * i
            b = jnp.where(reset, xi, xi * mult).reshape(groups, SUBLANES, LRU_BLOCK_W)
            a = jnp.where(reset, 0.0, a).reshape(groups, SUBLANES, LRU_BLOCK_W)
            for d in (1, 2, 4):
                keep = row8 >= d
                b = jnp.where(keep, a * pltpu.roll(b, d, 1) + b, b)
                a = jnp.where(keep, a * pltpu.roll(a, d, 1), a)
            a_s[:, sl] = a.reshape(tm, LRU_BLOCK_W)
            b_s[:, sl] = b.reshape(tm, LRU_BLOCK_W)
        return issued

    def recurrence():
        def group(gidx, h_prev):
            r0 = pl.multiple_of(gidx * SUBLANES, SUBLANES)
            h8 = b_s[pl.ds(r0, SUBLANES), :] + a_s[pl.ds(r0, SUBLANES), :] * h_prev
            b_s[pl.ds(r0, SUBLANES), :] = h8
            return jnp.broadcast_to(h8[SUBLANES - 1:SUBLANES, :], (SUBLANES, D_MODEL))

        hc_s[...] = lax.fori_loop(0, groups, group, hc_s[...], unroll=8)

    def project_group(c0):
        return project(wl_ref, c0), project(wgb_ref, c0), project(wga_ref, c0)

    n_group = D_MODEL // PROJ_CHUNK
    attn_after = n_group // 2 - 1
    cur = project_group(0)
    for k in range(n_group):
        c0 = k * PROJ_CHUNK
        if k == attn_after:
            issue_next = lambda: (project(wqd_ref, 0), project(wkv_ref, 0))
        elif k + 1 < n_group:
            issue_next = functools.partial(project_group, c0 + PROJ_CHUNK)
        else:
            issue_next = lambda: None
        nxt = recurrence_inputs(c0, cur[0], issue_next)
        gb_ref[0, :, c0:c0 + PROJ_CHUNK] = cur[1].astype(BF16)
        sg_s[:, c0:c0 + PROJ_CHUNK] = _sigmoid(cur[2])
        if k == attn_after:
            after = project_group(c0 + PROJ_CHUNK)
            queries(nxt[0])
            keys(nxt[1])
            nxt = after
        cur = nxt
    recurrence()
    yag_ref[0] = (sg_s[...] * b_s[...]).astype(BF16)


def _mixer_in(x, mod, pos3, cs, wins, convw, convb, wg, ba, bx, ap, qg, kvg, wqt, cst, *, tm):
    B, S, _ = x.shape
    row = lambda b, s: (b, s, 0)
    in_specs = [
        pl.BlockSpec((1, tm, D_MODEL), row),
        pl.BlockSpec((1, SUBLANES, D_MODEL), lambda b, s: (b, 0, 0)),
        pl.BlockSpec((1, tm, 1), row),
        pl.BlockSpec((1, tm, LANES), row),
        *[_const_spec(w.shape) for w in wins], _const_spec(convw.shape), _const_spec(convb.shape),
        _const_spec(wg.shape), _const_spec(ba.shape), _const_spec(bx.shape), _const_spec(ap.shape),
        _const_spec(qg.shape), _const_spec(kvg.shape), _const_spec(wqt.shape),
        pl.BlockSpec((LANES, tm), lambda b, s: (0, b * (S // tm) + s)),
    ]
    out_specs = [
        pl.BlockSpec((1, tm, D_MODEL), row),
        pl.BlockSpec((1, tm, D_MODEL), row),
        pl.BlockSpec((1, HEADS, HEAD_SLOT, tm), lambda b, s: (b, 0, 0, s)),
        pl.BlockSpec((1, tm, HEAD_SLOT), row),
        pl.BlockSpec((1, V_ROWS, tm), lambda b, s: (b, 0, s)),
    ]
    out_shape = [
        jax.ShapeDtypeStruct((B, S, D_MODEL), BF16),
        jax.ShapeDtypeStruct((B, S, D_MODEL), BF16),
        jax.ShapeDtypeStruct((B, HEADS, HEAD_SLOT, S), BF16),
        jax.ShapeDtypeStruct((B, S, HEAD_SLOT), BF16),
        jax.ShapeDtypeStruct((B, V_ROWS, S), BF16),
    ]
    scratch = [pltpu.VMEM((SUBLANES, D_MODEL), F32), pltpu.VMEM((tm, D_MODEL), F32),
               pltpu.VMEM((tm, D_MODEL), F32), pltpu.VMEM((tm, D_MODEL), F32),
               pltpu.VMEM((SUBLANES, D_MODEL), F32)]
    return pl.pallas_call(
        functools.partial(_mixer_in_kernel, tm=tm),
        grid=(B, S // tm), in_specs=in_specs, out_specs=out_specs, out_shape=out_shape,
        scratch_shapes=scratch,
        compiler_params=pltpu.CompilerParams(dimension_semantics=("arbitrary", "arbitrary"),
                                             vmem_limit_bytes=VMEM_LIMIT),
        name="mixer_in",
    )(x, mod, pos3, cs, *wins, convw, convb, wg, ba, bx, ap, qg, kvg, wqt, cst)


def _attn_kernel(q_ref, kk_ref, vt_ref, wuv_ref, o_ref,
                 sa_s, sb_s, sc_s, sd_s, mx_s, mx2_s, m_s, acc_s, *, tq):
    qi = pl.program_id(1)
    bufs = (sa_s, sb_s, sc_s, sd_s)

    def produce(j, h, buf, mask):
        k0 = pl.multiple_of(j * tq, tq)
        s = jnp.dot(kk_ref[0, pl.ds(k0, tq), :], q_ref[0, h], preferred_element_type=F32)
        if mask is not None:
            kidx = lax.broadcasted_iota(jnp.int32, (tq, tq), 0)
            qidx = lax.broadcasted_iota(jnp.int32, (tq, tq), 1)
            if mask == "any":
                qidx = qidx + (qi - j) * tq
            s = jnp.where(kidx <= qidx, s, -jnp.inf)
        buf[...] = s
        return jnp.max(s, axis=0, keepdims=True)

    def consume(j, h, buf, mx):
        k0 = pl.multiple_of(j * tq, tq)
        m = m_s[h]
        m_new = jnp.maximum(m, mx)
        m_s[h] = m_new
        alpha = jnp.exp2(m - m_new)
        p = jnp.exp2((buf[...] - m_new).astype(BF16))
        vt = vt_ref[0, :, pl.ds(k0, tq)]
        acc_s[h] = alpha * acc_s[h] + jnp.dot(vt, p, preferred_element_type=F32)

    def finalize(h):
        o = acc_s[h, 0:KV_RANK, :] * (1.0 / acc_s[h, KV_RANK:KV_RANK + 1, :])
        o_ref[0, h] = jnp.dot(o.T.astype(BF16), wuv_ref[h], preferred_element_type=F32).astype(BF16)

    def key_block(j, last):
        pending = [mx_s[...], mx2_s[...]]
        for h in range(HEADS):
            ahead = h + LOOKAHEAD
            if ahead < HEADS:
                mx_new = produce(j, ahead, bufs[ahead % len(bufs)], "diag" if last else None)
            elif not last:
                mx_new = produce(j + 1, ahead - HEADS, bufs[ahead % len(bufs)], "any")
            else:
                mx_new = None
            consume(j, h, bufs[h % len(bufs)], pending[0])
            pending = [pending[1], mx_new]
            if last and h > 0:
                finalize(h - 1)
        if last:
            finalize(HEADS - 1)
        else:
            mx_s[...] = pending[0]
            mx2_s[...] = pending[1]

    acc_s[...] = jnp.zeros(acc_s.shape, F32)
    m_s[...] = jnp.full(m_s.shape, -jnp.inf, F32)
    mx_s[...] = produce(0, 0, bufs[0], "any")
    mx2_s[...] = produce(0, 1, bufs[1], "any")

    def trip(t, c):
        key_block(2 * t, False)
        key_block(2 * t + 1, False)
        return c

    lax.fori_loop(0, qi // 2, trip, 0)

    @pl.when(qi % 2 == 1)
    def _():
        key_block(qi - 1, False)

    key_block(qi, True)


def _attention(q, kk, vt, wuv, *, tq):
    B, H, _, S = q.shape
    return pl.pallas_call(
        functools.partial(_attn_kernel, tq=tq),
        grid=(B, S // tq),
        in_specs=[pl.BlockSpec((1, H, HEAD_SLOT, tq), lambda b, i: (b, 0, 0, i)),
                  pl.BlockSpec((1, S, HEAD_SLOT), lambda b, i: (b, 0, 0)),
                  pl.BlockSpec((1, V_ROWS, S), lambda b, i: (b, 0, 0)),
                  _const_spec(wuv.shape)],
        out_specs=pl.BlockSpec((1, H, tq, V_HEAD), lambda b, i: (b, 0, i, 0)),
        out_shape=jax.ShapeDtypeStruct((B, H, S, V_HEAD), BF16),
        scratch_shapes=[pltpu.VMEM((tq, tq), F32), pltpu.VMEM((tq, tq), F32),
                        pltpu.VMEM((tq, tq), F32), pltpu.VMEM((tq, tq), F32),
                        pltpu.VMEM((1, tq), F32), pltpu.VMEM((1, tq), F32), pltpu.VMEM((H, 1, tq), F32),
                        pltpu.VMEM((H, V_ROWS, tq), F32)],
        compiler_params=pltpu.CompilerParams(
            dimension_semantics=("arbitrary", "arbitrary"), vmem_limit_bytes=VMEM_LIMIT),
        name="attention",
    )(q, kk, vt, wuv)


def _mixer_out_kernel(x_ref, yag_ref, gb_ref, yb_ref, mod_ref, wout_ref, wffi_ref, wffo_ref, fg_ref,
                      o_ref, *, final):
    mod = mod_ref[0]
    tm = x_ref.shape[1]
    halves = [slice(0, tm // 2), slice(tm // 2, tm)]

    def merge(rows):
        yb = jnp.concatenate([yb_ref[0, hh, rows, :] for hh in range(HEADS)], axis=1).astype(F32)
        y = yag_ref[0, rows, :].astype(F32) + _sigmoid(gb_ref[0, rows, :].astype(F32)) * yb
        return y.astype(BF16)

    ys = [merge(r) for r in halves]
    xs = [x_ref[0, r, :] + mod[2:3] * jnp.dot(y, wout_ref[...], preferred_element_type=F32)
          for r, y in zip(halves, ys)]
    hs = [(_rms(x) * (1.0 + mod[4:5]) + mod[3:4]).astype(BF16) for x in xs]
    gus = [jnp.dot(h, wffi_ref[...], preferred_element_type=F32) for h in hs]
    acts = [(_silu(gu[:, :D_FF]) * gu[:, D_FF:]).astype(BF16) for gu in gus]
    outs = [x + mod[5:6] * jnp.dot(act, wffo_ref[...], preferred_element_type=F32) for x, act in zip(xs, acts)]
    for r, x in zip(halves, outs):
        if final:
            x = _rms(x) * fg_ref[...]
        o_ref[0, r, :] = x


def _mixer_out(x, yag, gb, yb, mod, wout, wffi, wffo, fg, *, tm, final):
    B, S, _ = x.shape
    row = lambda b, s: (b, s, 0)
    return pl.pallas_call(
        functools.partial(_mixer_out_kernel, final=final),
        grid=(B, S // tm),
        in_specs=[pl.BlockSpec((1, tm, D_MODEL), row), pl.BlockSpec((1, tm, D_MODEL), row),
                  pl.BlockSpec((1, tm, D_MODEL), row),
                  pl.BlockSpec((1, HEADS, tm, V_HEAD), lambda b, s: (b, 0, s, 0)),
                  pl.BlockSpec((1, SUBLANES, D_MODEL), lambda b, s: (b, 0, 0)),
                  _const_spec(wout.shape), _const_spec(wffi.shape), _const_spec(wffo.shape),
                  _const_spec(fg.shape)],
        out_specs=pl.BlockSpec((1, tm, D_MODEL), row),
        out_shape=jax.ShapeDtypeStruct((B, S, D_MODEL), F32),
        compiler_params=pltpu.CompilerParams(dimension_semantics=("arbitrary", "arbitrary"),
                                             vmem_limit_bytes=VMEM_LIMIT),
        name="mixer_out",
    )(x, yag, gb, yb, mod, wout, wffi, wffo, fg)


def _rope_cols(w):
    half = QK_ROPE // 2
    x1, x2 = w[..., :half], w[..., half:]
    return jnp.concatenate([x1, x2, x2, x1], axis=-1)


def kernel(x, c, positions, w_ada, b_ada, w_in, conv_w, conv_b, lru_wa, lru_ba, lru_wx, lru_bx,
           lru_a_param, q_norm_g, kv_norm_g, w_uq, w_ukv, w_out, w_ffn_in, w_ffn_out, final_norm_g):
    B, S, _ = x.shape
    depth = w_ada.shape[0]
    tm = min(S, 512)
    tq = min(S, 512)

    c_pad = jnp.zeros((SUBLANES, D_MODEL), F32).at[:B].set(c)
    mod = _adaln_mod(c_pad, w_ada, b_ada)[:, :B].reshape(depth, B, 6, D_MODEL)
    mod = jnp.pad(mod, ((0, 0), (0, 0), (0, SUBLANES - 6), (0, 0)))

    inv_freq = ROPE_THETA ** (-jnp.arange(0, QK_ROPE, 2, dtype=F32) / QK_ROPE)
    rep = LANES // inv_freq.shape[0]
    cs, cst = _rope_tables(positions.astype(F32).reshape(-1, rep), jnp.tile(inv_freq, rep)[None, :])
    cs = cs.reshape(B, S, LANES)
    pos3 = positions.reshape(B, S, 1)

    wq_nope = jnp.transpose(w_uq[..., :QK_NOPE], (0, 2, 1, 3))
    wuk_t = jnp.transpose(w_ukv[..., :QK_NOPE], (0, 2, 3, 1))
    wq_abs = _absorb_q(wq_nope, wuk_t)
    wq_rope = _rope_cols(jnp.transpose(w_uq[..., QK_NOPE:], (0, 2, 1, 3)))
    wq_all = jnp.concatenate([wq_abs, wq_rope], axis=-1)
    wq_all_t = jnp.transpose(wq_all, (0, 1, 3, 2)).reshape(depth, HEADS * HEAD_SLOT, Q_RANK).astype(BF16)
    wuv = jnp.transpose(w_ukv[..., QK_NOPE:], (0, 2, 1, 3)).astype(BF16)

    o_q, o_kv, o_kr, o_ga, o_gb = (D_MODEL, D_MODEL + Q_RANK, D_MODEL + Q_RANK + KV_RANK,
                                   D_MODEL + Q_RANK + KV_RANK + QK_ROPE,
                                   2 * D_MODEL + Q_RANK + KV_RANK + QK_ROPE)
    wins = (w_in[..., :o_q].astype(BF16), w_in[..., o_ga:o_gb].astype(BF16), w_in[..., o_gb:].astype(BF16),
            w_in[..., o_q:o_kv].astype(BF16),
            jnp.concatenate([w_in[..., o_kv:o_kr], _rope_cols(w_in[..., o_kr:o_ga])], axis=-1).astype(BF16))
    wg = jnp.concatenate([lru_wa, lru_wx], axis=-1).astype(BF16)
    convw = jnp.pad(conv_w, ((0, 0), (0, SUBLANES - CONV_W), (0, 0)))
    wout = w_out.astype(BF16)
    wffi = w_ffn_in.astype(BF16)
    wffo = w_ffn_out.astype(BF16)
    fg = final_norm_g.reshape(1, D_MODEL)

    for l in range(depth):
        yag, gb, q, kk, vt = _mixer_in(
            x, mod[l], pos3, cs, [w[l] for w in wins], convw[l], conv_b[l].reshape(1, -1), wg[l],
            lru_ba[l].reshape(1, -1), lru_bx[l].reshape(1, -1), lru_a_param[l].reshape(1, -1),
            q_norm_g[l].reshape(1, -1), kv_norm_g[l].reshape(1, -1), wq_all_t[l], cst, tm=tm)
        yb = _attention(q, kk, vt, wuv[l], tq=tq)
        x = _mixer_out(x, yag, gb, yb, mod[l], wout[l], wffi[l], wffo[l], fg, tm=tm,
                       final=(l == depth - 1))
    return x
```
